```python
import math
import jax, jax.numpy as jnp
from jax import lax
import numpy as np

D_MODEL = 1024
BATCH = 1
SEQ = 16384
DEPTH = 2
DEC_BATCH = 128
DEC_SEQ = 8
PAST_LEN = 16384
PAGE_SIZE = 128

N_EVEN = (DEPTH + 1) // 2
N_ODD = DEPTH // 2
EPS = 1e-6
MOD_SLOTS = 6

A_HEAD_DIM = 64
A_HEADS = D_MODEL // (2 * A_HEAD_DIM)
A_KV_HEADS = 2
A_REP = A_HEADS // A_KV_HEADS
WINDOW = 128
A_QW = A_HEADS * A_HEAD_DIM
A_KVW = A_KV_HEADS * A_HEAD_DIM

D_INNER = D_MODEL // 2
B_HEAD_DIM = 64
B_HEADS = D_INNER // B_HEAD_DIM
B_GROUPS = 2
D_STATE = 128
D_CONV = 4
SSD_CHUNK = 128
XBC_DIM = D_INNER + 2 * B_GROUPS * D_STATE

IN_EVEN = A_QW + 2 * A_KVW + D_INNER + XBC_DIM + B_HEADS
SPLIT_EVEN = (A_QW, A_QW + A_KVW, A_QW + 2 * A_KVW, A_QW + 2 * A_KVW + D_INNER, A_QW + 2 * A_KVW + D_INNER + XBC_DIM)
MIX_EVEN = A_QW + D_INNER

C_HEADS = 16
C_NOPE = 64
C_ROPE = 32
C_V = 64
Q_LORA = 384
KV_LORA = 256
ROPE_THETA = 10000.0
MLA_SCALE = (C_NOPE + C_ROPE) ** -0.5
MLA_Q_BLOCK = 128
IN_ODD = Q_LORA + KV_LORA + C_ROPE

D_FF = 2816
N_EXPERTS = 8
TOP_K = 2
D_FF_EXPERT = 3584

kernel_name = 'hybrid_swa_ssd_mla_moe_adaln_step'

F32 = jnp.float32


def rms_norm(x, gain=None):
    xf = x.astype(F32)
    y = xf * lax.rsqrt(jnp.mean(xf * xf, axis=-1, keepdims=True) + EPS)
    if gain is not None:
        y = y * gain.astype(F32)
    return y.astype(x.dtype)


def ada_modulation(c, w, b):
    m = jax.nn.silu(c) @ w + b
    return jnp.split(m[:, None, :], MOD_SLOTS, axis=-1)


def modulate(x, shift, scale):
    return rms_norm(x) * (1 + scale) + shift


def alibi_slopes(n):
    return jnp.exp2(-8.0 * jnp.arange(1, n + 1, dtype=F32) / n)


def sink_window_attention(q, k, v, q_pos, k_pos, sinks):
    s = jnp.einsum('bnqgrd,bnkgd->bngrqk', q, k).astype(F32) * (A_HEAD_DIM ** -0.5)
    dist = q_pos[:, :, None] - k_pos[:, None, :]
    slopes = alibi_slopes(A_HEADS).reshape(A_KV_HEADS, A_REP)
    s = s - slopes[None, None, :, :, None, None] * dist.astype(F32)[None, :, None, None]
    valid = (dist >= 0) & (dist < WINDOW) & (k_pos[:, None, :] >= 0)
    s = jnp.where(valid[None, :, None, None], s, -jnp.inf)
    sink = jnp.broadcast_to(sinks.astype(F32).reshape(A_KV_HEADS, A_REP)[None, None, :, :, None, None], s.shape[:-1] + (1,))
    p = jax.nn.softmax(jnp.concatenate([s, sink], axis=-1), axis=-1)[..., :-1]
    return jnp.einsum('bngrqk,bnkgd->bnqgrd', p.astype(v.dtype), v)


def swa_prompt(q, k, v, sinks):
    b, s = q.shape[:2]
    nb = s // WINDOW
    qb = q.reshape(b, nb, WINDOW, A_KV_HEADS, A_REP, A_HEAD_DIM)

    def band(t):
        tb = t.reshape(b, nb, WINDOW, A_KV_HEADS, A_HEAD_DIM)
        prev = jnp.pad(tb, ((0, 0), (1, 0), (0, 0), (0, 0), (0, 0)))[:, :-1]
        return jnp.concatenate([prev, tb], axis=2)

    start = jnp.arange(nb, dtype=jnp.int32)[:, None] * WINDOW
    q_pos = start + jnp.arange(WINDOW, dtype=jnp.int32)[None]
    k_pos = start - WINDOW + jnp.arange(2 * WINDOW, dtype=jnp.int32)[None]
    o = sink_window_attention(qb, band(k), band(v), q_pos, k_pos, sinks)
    w = min(WINDOW, PAST_LEN)
    new_buf = jnp.stack([k[:, -w:], v[:, -w:]], axis=2)
    return o.reshape(b, s, A_QW), new_buf


def swa_sample(q, k, v, buf, sinks):
    b, l = q.shape[:2]
    w = buf.shape[1]
    kk = jnp.concatenate([buf[:, :, 0], k], axis=1)
    vv = jnp.concatenate([buf[:, :, 1], v], axis=1)
    q_pos = PAST_LEN + jnp.arange(l, dtype=jnp.int32)[None]
    k_pos = PAST_LEN - w + jnp.arange(w + l, dtype=jnp.int32)[None]
    o = sink_window_attention(q.reshape(b, 1, l, A_KV_HEADS, A_REP, A_HEAD_DIM), kk[:, None], vv[:, None], q_pos, k_pos, sinks)
    new_buf = jnp.concatenate([buf, jnp.stack([k, v], axis=2)], axis=1)[:, -w:]
    return o.reshape(b, l, A_QW), new_buf


def segsum(a):
    t = a.shape[-1]
    ae = jnp.broadcast_to(a[..., :, None], a.shape + (t,))
    ae = jnp.where(jnp.tril(jnp.ones((t, t), bool), -1), ae, 0)
    cs = jnp.cumsum(ae, axis=-2)
    return jnp.where(jnp.tril(jnp.ones((t, t), bool)), cs, -jnp.inf)


def ssd_chunked(x, dt, a, bm, cm, h0):
    bsz, l, h, p = x.shape
    q = math.gcd(l, SSD_CHUNK)
    nc = l // q
    rep = h // B_GROUPS
    bh = jnp.repeat(bm.astype(F32), rep, axis=2).reshape(bsz, nc, q, h, D_STATE)
    ch = jnp.repeat(cm.astype(F32), rep, axis=2).reshape(bsz, nc, q, h, D_STATE)
    xd = (x.astype(F32) * dt[..., None]).reshape(bsz, nc, q, h, p)
    adt = (a * dt).reshape(bsz, nc, q, h).transpose(0, 3, 1, 2)
    acs = jnp.cumsum(adt, axis=-1)
    lmat = jnp.exp(segsum(adt))
    y_diag = jnp.einsum('bclhn,bcshn,bhcls,bcshp->bclhp', ch, bh, lmat, xd)
    decay_states = jnp.exp(acs[..., -1:] - acs)
    states = jnp.einsum('bclhn,bhcl,bclhp->bchpn', bh, decay_states, xd)
    states = jnp.concatenate([h0.astype(F32)[:, None], states], axis=1)
    decay_chunk = jnp.exp(segsum(jnp.pad(acs[..., -1], ((0, 0), (0, 0), (1, 0)))))
    new_states = jnp.einsum('bhzc,bchpn->bzhpn', decay_chunk, states)
    states, h_last = new_states[:, :-1], new_states[:, -1]
    y_off = jnp.einsum('bclhn,bchpn,bhcl->bclhp', ch, states, jnp.exp(acs))
    return (y_diag + y_off).reshape(bsz, l, h, p), h_last


def causal_depthwise_conv(xbc, conv_prev, w, bias):
    l = xbc.shape[1]
    xpad = jnp.concatenate([conv_prev, xbc], axis=1)
    out = bias
    for j in range(D_CONV):
        out = out + xpad[:, j:j + l] * w[j]
    return out, xpad[:, l:]


def ssd_branch(xbc, z, dt_raw, conv_prev, h0, p):
    b, l = xbc.shape[:2]
    conv, new_conv = causal_depthwise_conv(xbc, conv_prev, p['conv_w'], p['conv_b'])
    conv = jax.nn.silu(conv)
    xs, bm, cm = jnp.split(conv, (D_INNER, D_INNER + B_GROUPS * D_STATE), axis=-1)
    xs = xs.reshape(b, l, B_HEADS, B_HEAD_DIM)
    bm = bm.reshape(b, l, B_GROUPS, D_STATE)
    cm = cm.reshape(b, l, B_GROUPS, D_STATE)
    dt = jax.nn.softplus(dt_raw.astype(F32) + p['dt_bias'].astype(F32))
    a = -jnp.exp(p['a_log'].astype(F32))
    y, h_last = ssd_chunked(xs, dt, a, bm, cm, h0)
    y = y + p['d_skip'].astype(F32)[:, None] * xs.astype(F32)
    y = y.reshape(b, l, D_INNER).astype(z.dtype)
    g = (y * jax.nn.silu(z)).reshape(b, l, B_GROUPS, D_INNER // B_GROUPS)
    y = rms_norm(g).reshape(b, l, D_INNER) * p['gnorm']
    return y, new_conv, h_last


def rope(t, pos):
    half = C_ROPE // 2
    freqs = ROPE_THETA ** (-2.0 * jnp.arange(half, dtype=F32) / C_ROPE)
    ang = pos.astype(F32)[:, None] * freqs[None]
    shape = (1, pos.shape[0]) + (1,) * (t.ndim - 3) + (half,)
    cos = jnp.cos(ang).reshape(shape)
    sin = jnp.sin(ang).reshape(shape)
    tf = t.astype(F32)
    t1, t2 = tf[..., :half], tf[..., half:]
    return jnp.concatenate([t1 * cos - t2 * sin, t1 * sin + t2 * cos], axis=-1).astype(t.dtype)


def mla_project(h, pos, p):
    b, l = h.shape[:2]
    cq, ckv, kpe = jnp.split(h @ p['w_in'], (Q_LORA, Q_LORA + KV_LORA), axis=-1)
    cq = rms_norm(cq, p['qnorm'])
    ckv = rms_norm(ckv, p['kvnorm'])
    q = (cq @ p['w_uq']).reshape(b, l, C_HEADS, C_NOPE + C_ROPE)
    q_nope = q[..., :C_NOPE]
    q_pe = rope(q[..., C_NOPE:], pos)
    kpe = rope(kpe, pos)
    return q_nope, q_pe, ckv, kpe


def mla_prompt_attention(q_nope, q_pe, ckv, kpe, w_uk, w_uv):
    b, s = q_nope.shape[:2]
    nb = s // MLA_Q_BLOCK
    k_nope = jnp.einsum('bsc,chd->bshd', ckv, w_uk)
    v = jnp.einsum('bsc,chd->bshd', ckv, w_uv)
    k_pos = jnp.arange(s, dtype=jnp.int32)

    def block(args):
        qn, qp, start = args
        sc = (jnp.einsum('bqhd,bkhd->bhqk', qn, k_nope) + jnp.einsum('bqhr,bkr->bhqk', qp, kpe)).astype(F32) * MLA_SCALE
        q_pos = start + jnp.arange(MLA_Q_BLOCK, dtype=jnp.int32)
        sc = jnp.where(k_pos[None, :] <= q_pos[:, None], sc, -jnp.inf)
        pr = jax.nn.softmax(sc, axis=-1)
        return jnp.einsum('bhqk,bkhd->bqhd', pr.astype(v.dtype), v)

    def to_blocks(t):
        return jnp.swapaxes(t.reshape((b, nb, MLA_Q_BLOCK) + t.shape[2:]), 0, 1)

    o = lax.map(block, (to_blocks(q_nope), to_blocks(q_pe), jnp.arange(nb, dtype=jnp.int32) * MLA_Q_BLOCK))
    return jnp.swapaxes(o, 0, 1).reshape(b, s, C_HEADS * C_V)


def mla_sample_attention(q_nope, q_pe, ckv_new, kpe_new, cache_ckv, cache_kpe, page_table, layer, w_uk, w_uv):
    b, l = q_nope.shape[:2]
    past = page_table.shape[1] * PAGE_SIZE
    q_lat = jnp.einsum('blhd,chd->blhc', q_nope, w_uk)
    k_pos = jnp.arange(past + l, dtype=jnp.int32)
    q_pos = past + jnp.arange(l, dtype=jnp.int32)
    causal = k_pos[None, :] <= q_pos[:, None]

    def one(args):
        ql, qp, cn, kn, pages = args
        ck = jnp.concatenate([cache_ckv[layer, pages].reshape(past, KV_LORA), cn], axis=0)
        kp = jnp.concatenate([cache_kpe[layer, pages].reshape(past, C_ROPE), kn], axis=0)
        sc = (jnp.einsum('lhc,kc->hlk', ql, ck) + jnp.einsum('lhr,kr->hlk', qp, kp)).astype(F32) * MLA_SCALE
        sc = jnp.where(causal[None], sc, -jnp.inf)
        pr = jax.nn.softmax(sc, axis=-1)
        return jnp.einsum('hlk,kc->lhc', pr.astype(ck.dtype), ck)

    o_lat = lax.map(one, (q_lat, q_pe, ckv_new, kpe_new, page_table))
    return jnp.einsum('blhc,chd->blhd', o_lat, w_uv).reshape(b, l, C_HEADS * C_V)


def swiglu(h, w_gu, w_down):
    g, u = jnp.split(h @ w_gu, 2, axis=-1)
    return (jax.nn.silu(g) * u) @ w_down


def moe_swiglu(h, router, w_gu, w_down):
    logits = (h @ router).astype(F32)
    top_v, top_i = lax.top_k(logits, TOP_K)
    gates = jax.nn.softmax(top_v, axis=-1)
    dense_gate = jnp.einsum('blk,blke->ble', gates, jax.nn.one_hot(top_i, N_EXPERTS, dtype=F32)).astype(h.dtype)
    out = jnp.zeros_like(h)
    for e in range(N_EXPERTS):
        out = out + dense_gate[..., e:e + 1] * swiglu(h, w_gu[e], w_down[e])
    return out


def even_layer(x, c, swa_buf, conv_prev, h0, p):
    sh1, sc1, g1, sh2, sc2, g2 = ada_modulation(c, p['mod_w'], p['mod_b'])
    b, l = x.shape[:2]
    h = modulate(x, sh1, sc1)
    q, k, v, z, xbc, dt = jnp.split(h @ p['w_in'], SPLIT_EVEN, axis=-1)
    q = q.reshape(b, l, A_HEADS, A_HEAD_DIM)
    k = k.reshape(b, l, A_KV_HEADS, A_HEAD_DIM)
    v = v.reshape(b, l, A_KV_HEADS, A_HEAD_DIM)
    if swa_buf is None:
        attn, new_buf = swa_prompt(q, k, v, p['sinks'])
        conv_prev = jnp.zeros((b, D_CONV - 1, XBC_DIM), x.dtype)
        h0 = jnp.zeros((b, B_HEADS, B_HEAD_DIM, D_STATE), F32)
    else:
        attn, new_buf = swa_sample(q, k, v, swa_buf, p['sinks'])
    ssm, new_conv, h_last = ssd_branch(xbc, z, dt, conv_prev, h0, p)
    x = x + g1 * (jnp.concatenate([attn, ssm], axis=-1) @ p['w_out'])
    h = modulate(x, sh2, sc2)
    x = x + g2 * swiglu(h, p['w_gu'], p['w_down'])
    return x, new_buf, new_conv, h_last.astype(x.dtype)


def odd_layer(x, c, paged, p):
    sh1, sc1, g1, sh2, sc2, g2 = ada_modulation(c, p['mod_w'], p['mod_b'])
    l = x.shape[1]
    h = modulate(x, sh1, sc1)
    if paged is None:
        pos = jnp.arange(l, dtype=jnp.int32)
        q_nope, q_pe, ckv, kpe = mla_project(h, pos, p)
        attn = mla_prompt_attention(q_nope, q_pe, ckv, kpe, p['w_uk'], p['w_uv'])
    else:
        cache_ckv, cache_kpe, page_table, layer = paged
        pos = PAST_LEN + jnp.arange(l, dtype=jnp.int32)
        q_nope, q_pe, ckv, kpe = mla_project(h, pos, p)
        attn = mla_sample_attention(q_nope, q_pe, ckv, kpe, cache_ckv, cache_kpe, page_table, layer, p['w_uk'], p['w_uv'])
    x = x + g1 * (attn @ p['w_out'])
    h = modulate(x, sh2, sc2)
    x = x + g2 * moe_swiglu(h, p['router'], p['w_gu'], p['w_down'])
    return x, ckv, kpe


def setup_inputs(seed: int = 0) -> dict:
    key = jax.random.key(seed)
    ks = iter(jax.random.split(key, 48))

    def nrm(shape, scale=1.0):
        return jax.random.normal(next(ks), shape, F32) * scale

    n_pages = PAST_LEN // PAGE_SIZE
    n_pool = (DEC_BATCH * n_pages * 5 + 3) // 4
    w_buf = min(WINDOW, PAST_LEN)
    d = D_MODEL
    inp = {}
    inp['x_prompt'] = nrm((BATCH, SEQ, d))
    inp['x_sample'] = nrm((DEC_BATCH, DEC_SEQ, d))
    inp['c_prompt'] = nrm((BATCH, d))
    inp['c_sample'] = nrm((DEC_BATCH, d))
    inp['state_swa_kv'] = nrm((N_EVEN, DEC_BATCH, w_buf, 2, A_KV_HEADS, A_HEAD_DIM))
    inp['state_conv'] = nrm((N_EVEN, DEC_BATCH, D_CONV - 1, XBC_DIM))
    inp['state_ssm'] = nrm((N_EVEN, DEC_BATCH, B_HEADS, B_HEAD_DIM, D_STATE), 0.3)
    inp['cache_ckv'] = nrm((N_ODD, n_pool, PAGE_SIZE, KV_LORA))
    inp['cache_kpe'] = nrm((N_ODD, n_pool, PAGE_SIZE, C_ROPE))
    perm = jax.random.permutation(next(ks), n_pool)
    inp['page_table'] = perm[:DEC_BATCH * n_pages].reshape(DEC_BATCH, n_pages).astype(jnp.int32)
    inp['ev_mod_w'] = nrm((N_EVEN, d, MOD_SLOTS * d), 0.5 * d ** -0.5)
    inp['ev_mod_b'] = nrm((N_EVEN, MOD_SLOTS * d), 0.02)
    inp['ev_w_in'] = nrm((N_EVEN, d, IN_EVEN), d ** -0.5)
    inp['ev_sinks'] = nrm((N_EVEN, A_HEADS), 0.5)
    inp['ev_conv_w'] = nrm((N_EVEN, D_CONV, XBC_DIM), D_CONV ** -0.5)
    inp['ev_conv_b'] = nrm((N_EVEN, XBC_DIM), 0.02)
    u = jax.random.uniform(next(ks), (N_EVEN, B_HEADS), F32)
    dt0 = jnp.exp(u * (math.log(0.1) - math.log(0.001)) + math.log(0.001))
    inp['ev_dt_bias'] = dt0 + jnp.log(-jnp.expm1(-dt0))
    inp['ev_a_log'] = jnp.log(jax.random.uniform(next(ks), (N_EVEN, B_HEADS), F32, 1.0, 16.0))
    inp['ev_d_skip'] = 1.0 + nrm((N_EVEN, B_HEADS), 0.1)
    inp['ev_gnorm'] = 1.0 + nrm((N_EVEN, D_INNER), 0.1)
    inp['ev_w_out'] = nrm((N_EVEN, MIX_EVEN, d), MIX_EVEN ** -0.5)
    inp['ev_w_gu'] = nrm((N_EVEN, d, 2 * D_FF), d ** -0.5)
    inp['ev_w_down'] = nrm((N_EVEN, D_FF, d), D_FF ** -0.5)
    inp['od_mod_w'] = nrm((N_ODD, d, MOD_SLOTS * d), 0.5 * d ** -0.5)
    inp['od_mod_b'] = nrm((N_ODD, MOD_SLOTS * d), 0.02)
    inp['od_w_in'] = nrm((N_ODD, d, IN_ODD), d ** -0.5)
    inp['od_qnorm'] = 1.0 + nrm((N_ODD, Q_LORA), 0.1)
    inp['od_kvnorm'] = 1.0 + nrm((N_ODD, KV_LORA), 0.1)
    inp['od_w_uq'] = nrm((N_ODD, Q_LORA, C_HEADS * (C_NOPE + C_ROPE)), Q_LORA ** -0.5)
    inp['od_w_uk'] = nrm((N_ODD, KV_LORA, C_HEADS, C_NOPE), KV_LORA ** -0.5)
    inp['od_w_uv'] = nrm((N_ODD, KV_LORA, C_HEADS, C_V), KV_LORA ** -0.5)
    inp['od_w_out'] = nrm((N_ODD, C_HEADS * C_V, d), (C_HEADS * C_V) ** -0.5)
    inp['od_router'] = nrm((N_ODD, d, N_EXPERTS), d ** -0.5)
    inp['od_w_gu'] = nrm((N_ODD, N_EXPERTS, d, 2 * D_FF_EXPERT), d ** -0.5)
    inp['od_w_down'] = nrm((N_ODD, N_EXPERTS, D_FF_EXPERT, d), D_FF_EXPERT ** -0.5)
    inp['final_norm'] = 1.0 + nrm((d,), 0.1)
    return inp


def reference(x_prompt, x_sample, c_prompt, c_sample, state_swa_kv, state_conv, state_ssm, cache_ckv, cache_kpe, page_table,
              ev_mod_w, ev_mod_b, ev_w_in, ev_sinks, ev_conv_w, ev_conv_b, ev_dt_bias, ev_a_log, ev_d_skip, ev_gnorm,
              ev_w_out, ev_w_gu, ev_w_down, od_mod_w, od_mod_b, od_w_in, od_qnorm, od_kvnorm, od_w_uq, od_w_uk, od_w_uv,
              od_w_out, od_router, od_w_gu, od_w_down, final_norm):
    xp, xs = x_prompt, x_sample
    swa_p, swa_s, conv_p, conv_s, ssm_p, ssm_s = [], [], [], [], [], []
    ckv_p, ckv_s, kpe_p, kpe_s = [], [], [], []
    for layer in range(DEPTH):
        i = layer // 2
        if layer % 2 == 0:
            p = dict(mod_w=ev_mod_w[i], mod_b=ev_mod_b[i], w_in=ev_w_in[i], sinks=ev_sinks[i], conv_w=ev_conv_w[i],
                     conv_b=ev_conv_b[i], dt_bias=ev_dt_bias[i], a_log=ev_a_log[i], d_skip=ev_d_skip[i],
                     gnorm=ev_gnorm[i], w_out=ev_w_out[i], w_gu=ev_w_gu[i], w_down=ev_w_down[i])
            xp, nb_, nc_, nh_ = even_layer(xp, c_prompt, None, None, None, p)
            swa_p.append(nb_); conv_p.append(nc_); ssm_p.append(nh_)
            xs, nb_, nc_, nh_ = even_layer(xs, c_sample, state_swa_kv[i], state_conv[i], state_ssm[i], p)
            swa_s.append(nb_); conv_s.append(nc_); ssm_s.append(nh_)
        else:
            p = dict(mod_w=od_mod_w[i], mod_b=od_mod_b[i], w_in=od_w_in[i], qnorm=od_qnorm[i], kvnorm=od_kvnorm[i],
                     w_uq=od_w_uq[i], w_uk=od_w_uk[i], w_uv=od_w_uv[i], w_out=od_w_out[i], router=od_router[i],
                     w_gu=od_w_gu[i], w_down=od_w_down[i])
            xp, ck_, kp_ = odd_layer(xp, c_prompt, None, p)
            ckv_p.append(ck_); kpe_p.append(kp_)
            xs, ck_, kp_ = odd_layer(xs, c_sample, (cache_ckv, cache_kpe, page_table, i), p)
            ckv_s.append(ck_); kpe_s.append(kp_)
    y_prompt = rms_norm(xp, final_norm)
    y_sample = rms_norm(xs, final_norm)
    swa_kv_prompt = jnp.stack(swa_p)
    swa_kv_sample = jnp.stack(swa_s)
    conv_prompt = jnp.stack(conv_p)
    conv_sample = jnp.stack(conv_s)
    ssm_prompt = jnp.stack(ssm_p)
    ssm_sample = jnp.stack(ssm_s)
    ckv_prompt = jnp.stack(ckv_p)
    ckv_sample = jnp.stack(ckv_s)
    kpe_prompt = jnp.stack(kpe_p)
    kpe_sample = jnp.stack(kpe_s)
    return (y_prompt, y_sample, swa_kv_prompt, swa_kv_sample, conv_prompt, conv_sample, ssm_prompt, ssm_sample, ckv_prompt, ckv_sample, kpe_prompt, kpe_sample)
```

```python
import functools
import math

import jax
import jax.numpy as jnp
from jax import lax
from jax.experimental import pallas as pl
from jax.experimental.pallas import tpu as pltpu

F32 = jnp.float32
BF16 = jnp.bfloat16

D_MODEL = 1024
EPS = 1e-6
MOD_SLOTS = 6

A_HEAD_DIM = 64
A_HEADS = 8
A_KV_HEADS = 2
A_REP = A_HEADS // A_KV_HEADS
WINDOW = 128
A_QW = A_HEADS * A_HEAD_DIM
A_KVW = A_KV_HEADS * A_HEAD_DIM

D_INNER = 512
B_HEAD_DIM = 64
B_HEADS = 8
B_GROUPS = 2
D_STATE = 128
D_CONV = 4
SSD_CHUNK = 128
XBC_DIM = D_INNER + 2 * B_GROUPS * D_STATE

C_HEADS = 16
C_NOPE = 64
C_ROPE = 32
C_V = 64
Q_LORA = 384
KV_LORA = 256
ROPE_THETA = 10000.0
MLA_SCALE = (C_NOPE + C_ROPE) ** -0.5
PAGE_SIZE = 128

D_FF = 2816
N_EXPERTS = 8
D_FF_EXPERT = 3584

LANE = 128
VMEM_LIMIT = 56 * 1024 * 1024

_HI = lax.Precision.HIGHEST
_NT = (((1,), (1,)), ((), ()))
_TN = (((0,), (0,)), ((), ()))


def _params(*sem):
    return pltpu.CompilerParams(dimension_semantics=sem, vmem_limit_bytes=VMEM_LIMIT)


def _tile(n, pref):
    t = min(n, pref)
    while n % t:
        t -= 8
    return t


def _rms(x):
    return x * lax.rsqrt(jnp.mean(x * x, axis=-1, keepdims=True) + EPS)


def _silu(x):
    return x * jax.nn.sigmoid(x)


def _softplus(x):
    e = jnp.exp(-jnp.abs(x))
    u = 1.0 + e
    lg = jnp.where(u == 1.0, e, jnp.log(u) * e / (u - 1.0))
    return jnp.maximum(x, 0.0) + lg


def _row_spec(arr, tm):
    d = arr.shape[1]
    if arr.shape[0] == 1:
        return pl.BlockSpec((1, d), lambda i: (0, 0))
    return pl.BlockSpec((tm, d), lambda i: (i, 0))


def _const_spec(arr):
    nd = arr.ndim
    return pl.BlockSpec(arr.shape, lambda *_: (0,) * nd)


def _mod_kernel(c_ref, w_ref, b_ref, o_ref):
    s = _silu(c_ref[...]).astype(BF16)
    o_ref[...] = jnp.dot(s, w_ref[...].astype(BF16), preferred_element_type=F32) + b_ref[...]


def _ada_mod(c_all, w, b):
    r, d = c_all.shape
    n = w.shape[1]
    tn = 1024
    return pl.pallas_call(
        _mod_kernel,
        grid=(n // tn,),
        in_specs=[pl.BlockSpec((r, d), lambda j: (0, 0)),
                  pl.BlockSpec((d, tn), lambda j: (0, j)),
                  pl.BlockSpec((1, tn), lambda j: (0, j))],
        out_specs=pl.BlockSpec((r, tn), lambda j: (0, j)),
        out_shape=jax.ShapeDtypeStruct((r, n), F32),
        compiler_params=_params("arbitrary"),
        name="ada_mod",
    )(c_all, w, b.reshape(1, n))


def _ev_in_kernel(x_ref, sh_ref, sc_ref, w_ref, q_ref, z_ref, xbc_ref, kv_ref, dt_ref):
    x = x_ref[...]
    h = _rms(x) * (1.0 + sc_ref[...]) + sh_ref[...]
    y = jnp.dot(h.astype(BF16), w_ref[...], preferred_element_type=F32)
    q_ref[...] = y[:, 0:A_QW].astype(BF16)
    z_ref[...] = y[:, A_QW:A_QW + D_INNER]
    xbc_ref[...] = y[:, 1024:1024 + XBC_DIM]
    kv_ref[...] = y[:, 2048:2048 + 2 * A_KVW]
    dt_ref[...] = y[:, 2304:2304 + LANE]


def _ev_in(x, sh, sc, w):
    t, d = x.shape
    tm = _tile(t, 512)
    outs = [(A_QW, BF16), (D_INNER, F32), (XBC_DIM, F32), (2 * A_KVW, F32), (LANE, F32)]
    return pl.pallas_call(
        _ev_in_kernel,
        grid=(t // tm,),
        in_specs=[pl.BlockSpec((tm, d), lambda i: (i, 0)), _row_spec(sh, tm), _row_spec(sc, tm), _const_spec(w)],
        out_specs=[pl.BlockSpec((tm, n), lambda i: (i, 0)) for n, _ in outs],
        out_shape=[jax.ShapeDtypeStruct((t, n), dt) for n, dt in outs],
        compiler_params=_params("arbitrary"),
        name="ev_in_proj",
    )(x, sh, sc, w)


def _swa_heads(q, k_all, v_all, valid, distf, sink_ref):
    outs = []
    for g in range(A_KV_HEADS):
        k_g = k_all[:, A_HEAD_DIM * g:A_HEAD_DIM * (g + 1)].astype(BF16)
        v_g = v_all[:, A_HEAD_DIM * g:A_HEAD_DIM * (g + 1)].astype(BF16)
        for r in range(A_REP):
            h = g * A_REP + r
            q_h = q[:, A_HEAD_DIM * h:A_HEAD_DIM * (h + 1)]
            s = lax.dot_general(q_h, k_g, _NT, preferred_element_type=F32) * (A_HEAD_DIM ** -0.5)
            s = s - (2.0 ** (-8.0 * (h + 1) / A_HEADS)) * distf
            s = jnp.where(valid, s, -jnp.inf)
            sink = sink_ref[h]
            m = jnp.maximum(jnp.max(s, axis=-1, keepdims=True), sink)
            p = jnp.exp(s - m)
            den = jnp.sum(p, axis=-1, keepdims=True) + jnp.exp(sink - m)
            outs.append(jnp.dot(p.astype(BF16), v_g, preferred_element_type=F32) / den)
    return jnp.concatenate(outs, axis=-1)


def _swa_prompt_kernel(sink_ref, q_ref, kvp_ref, kvc_ref, o_ref):
    i = pl.program_id(0)
    kv = jnp.concatenate([kvp_ref[...], kvc_ref[...]], axis=0)
    row = lax.broadcasted_iota(jnp.int32, (WINDOW, 2 * WINDOW), 0)
    col = lax.broadcasted_iota(jnp.int32, (WINDOW, 2 * WINDOW), 1)
    dist = row + WINDOW - col
    first_key = jnp.where(i > 0, 0, WINDOW)
    valid = (dist >= 0) & (dist < WINDOW) & (col >= first_key)
    o = _swa_heads(q_ref[...], kv[:, :A_KVW], kv[:, A_KVW:], valid, dist.astype(F32), sink_ref)
    o_ref[...] = o.astype(BF16)


def _swa_prompt(q, kv, sinks):
    t = q.shape[0]
    nb = t // WINDOW
    return pl.pallas_call(
        _swa_prompt_kernel,
        grid=(nb,),
        in_specs=[pl.BlockSpec(memory_space=pltpu.SMEM),
                  pl.BlockSpec((WINDOW, A_QW), lambda i: (i, 0)),
                  pl.BlockSpec((WINDOW, 2 * A_KVW), lambda i: (jnp.maximum(i - 1, 0), 0)),
                  pl.BlockSpec((WINDOW, 2 * A_KVW), lambda i: (i, 0))],
        out_specs=pl.BlockSpec((WINDOW, A_QW), lambda i: (i, 0)),
        out_shape=jax.ShapeDtypeStruct((t, A_QW), BF16),
        compiler_params=_params("arbitrary"),
        name="swa_prompt",
    )(sinks, q, kv, kv)


def _swa_sample_kernel(sink_ref, q_ref, kvn_ref, buf_ref, o_ref, nbuf_ref, *, bb, l):
    w = WINDOW
    row = lax.broadcasted_iota(jnp.int32, (l, 2 * w), 0)
    col = lax.broadcasted_iota(jnp.int32, (l, 2 * w), 1)
    dist = row + w - col
    valid = (dist >= 0) & (dist < w)
    distf = dist.astype(F32)
    for b in range(bb):
        buf = buf_ref[b]
        kvn = kvn_ref[b]
        kv = jnp.concatenate([buf, kvn, jnp.zeros((w - l, 2 * A_KVW), F32)], axis=0)
        o = _swa_heads(q_ref[b], kv[:, :A_KVW], kv[:, A_KVW:], valid, distf, sink_ref)
        o_ref[b] = o.astype(BF16)
        nbuf_ref[b, 0:w - l, :] = buf[l:, :]
        nbuf_ref[b, w - l:w, :] = kvn


def _swa_sample(q, kvn, buf, sinks):
    b, l, _ = q.shape
    bb = _tile(b, 8)
    return pl.pallas_call(
        functools.partial(_swa_sample_kernel, bb=bb, l=l),
        grid=(b // bb,),
        in_specs=[pl.BlockSpec(memory_space=pltpu.SMEM),
                  pl.BlockSpec((bb, l, A_QW), lambda i: (i, 0, 0)),
                  pl.BlockSpec((bb, l, 2 * A_KVW), lambda i: (i, 0, 0)),
                  pl.BlockSpec((bb, WINDOW, 2 * A_KVW), lambda i: (i, 0, 0))],
        out_specs=[pl.BlockSpec((bb, l, A_QW), lambda i: (i, 0, 0)),
                   pl.BlockSpec((bb, WINDOW, 2 * A_KVW), lambda i: (i, 0, 0))],
        out_shape=[jax.ShapeDtypeStruct((b, l, A_QW), BF16),
                   jax.ShapeDtypeStruct((b, WINDOW, 2 * A_KVW), F32)],
        compiler_params=_params("arbitrary"),
        name="swa_sample",
    )(sinks, q, kvn, buf)


def _ssd_chunk(conv, z, dt, state, a_row, dsk, gn):
    L = SSD_CHUNK
    gw = D_INNER // B_GROUPS
    hpg = B_HEADS // B_GROUPS
    xs = conv[:, :D_INNER]
    adt = a_row * dt
    r2 = lax.broadcasted_iota(jnp.int32, (L, L), 0)
    c2 = lax.broadcasted_iota(jnp.int32, (L, L), 1)
    lower = r2 >= c2
    acs = jnp.dot(lower.astype(F32), adt, precision=_HI, preferred_element_type=F32)
    acs_t = acs.T
    band = lax.broadcasted_iota(jnp.int32, (L, gw), 1) // B_HEAD_DIM

    ys, states = [], []
    for g in range(B_GROUPS):
        b_g = conv[:, D_INNER + D_STATE * g:D_INNER + D_STATE * (g + 1)].astype(BF16)
        c_g = conv[:, D_INNER + B_GROUPS * D_STATE + D_STATE * g:
                   D_INNER + B_GROUPS * D_STATE + D_STATE * (g + 1)].astype(BF16)
        xs_g = xs[:, gw * g:gw * (g + 1)]

        def expand(mat, g=g):
            out = jnp.broadcast_to(mat[:, hpg * g:hpg * g + 1], (L, gw))
            for r in range(1, hpg):
                out = jnp.where(band == r, jnp.broadcast_to(mat[:, hpg * g + r:hpg * g + r + 1], (L, gw)), out)
            return out

        acs_e = expand(acs)
        xd = xs_g * expand(dt)
        xd_b = xd.astype(BF16)
        gmat = lax.dot_general(c_g, b_g, _NT, preferred_element_type=F32)
        y_diag = None
        for r in range(hpg):
            h = hpg * g + r
            diff = acs[:, h:h + 1] - acs_t[h:h + 1, :]
            lmat = jnp.exp(jnp.where(lower, diff, -jnp.inf))
            yr = jnp.dot((gmat * lmat).astype(BF16), xd_b, preferred_element_type=F32)
            y_diag = yr if y_diag is None else jnp.where(band == r, yr, y_diag)
        s_g = state[gw * g:gw * (g + 1), :]
        y_off = lax.dot_general(c_g, s_g.astype(BF16), _NT, preferred_element_type=F32) * jnp.exp(acs_e)
        decay = jnp.exp(acs_e[L - 1:L, :] - acs_e)
        upd = lax.dot_general((xd * decay).astype(BF16), b_g, _TN, preferred_element_type=F32)
        dec_rows = jnp.concatenate(
            [jnp.broadcast_to(jnp.exp(acs_t[hpg * g + r:hpg * g + r + 1, L - 1:L]), (B_HEAD_DIM, D_STATE))
             for r in range(hpg)], axis=0)
        states.append(s_g * dec_rows + upd)
        ys.append(y_diag + y_off + dsk[:, gw * g:gw * (g + 1)] * xs_g)

    outs = []
    for g in range(B_GROUPS):
        gt = ys[g] * _silu(z[:, gw * g:gw * (g + 1)])
        outs.append(_rms(gt))
    y = jnp.concatenate(outs, axis=-1) * gn
    return y, jnp.concatenate(states, axis=0)


def _conv_from_pad(xp_ref, cw_ref, cb_ref):
    L = SSD_CHUNK
    acc = cb_ref[...] + cw_ref[D_CONV - 1:D_CONV, :] * xp_ref[8:8 + L, :]
    for j in range(D_CONV - 1):
        acc = acc + cw_ref[j:j + 1, :] * xp_ref[5 + j:5 + j + L, :]
    return _silu(acc)


def _ssd_prompt_kernel(xbc_ref, z_ref, dt_ref, cw_ref, cb_ref, dtb_ref, a_ref, dsk_ref, gn_ref,
                       y_ref, st_ref, xp_ref, s_ref):
    i = pl.program_id(0)
    L = SSD_CHUNK

    @pl.when(i == 0)
    def _():
        xp_ref[0:8, :] = jnp.zeros((8, XBC_DIM), F32)
        s_ref[...] = jnp.zeros_like(s_ref)

    xp_ref[8:8 + L, :] = xbc_ref[...]
    conv = _conv_from_pad(xp_ref, cw_ref, cb_ref)
    xp_ref[0:8, :] = xp_ref[L:L + 8, :]
    dt = _softplus(dt_ref[...] + dtb_ref[...])
    y, new_state = _ssd_chunk(conv, z_ref[...], dt, s_ref[...], a_ref[...], dsk_ref[...], gn_ref[...])
    s_ref[...] = new_state
    y_ref[...] = y.astype(BF16)

    @pl.when(i == pl.num_programs(0) - 1)
    def _():
        st_ref[...] = new_state


def _ssd_prompt(xbc, z, dt, cw, cb, dtb, a_row, dsk, gn):
    t = xbc.shape[0]
    L = SSD_CHUNK
    consts = [cw, cb, dtb, a_row, dsk, gn]
    return pl.pallas_call(
        _ssd_prompt_kernel,
        grid=(t // L,),
        in_specs=[pl.BlockSpec((L, XBC_DIM), lambda i: (i, 0)),
                  pl.BlockSpec((L, D_INNER), lambda i: (i, 0)),
                  pl.BlockSpec((L, LANE), lambda i: (i, 0))] + [_const_spec(c) for c in consts],
        out_specs=[pl.BlockSpec((L, D_INNER), lambda i: (i, 0)),
                   pl.BlockSpec((B_HEADS * B_HEAD_DIM, D_STATE), lambda i: (0, 0))],
        out_shape=[jax.ShapeDtypeStruct((t, D_INNER), BF16),
                   jax.ShapeDtypeStruct((B_HEADS * B_HEAD_DIM, D_STATE), F32)],
        scratch_shapes=[pltpu.VMEM((L + 8, XBC_DIM), F32), pltpu.VMEM((B_HEADS * B_HEAD_DIM, D_STATE), F32)],
        compiler_params=_params("arbitrary"),
        name="ssd_prompt",
    )(xbc, z, dt, *consts)


def _ssd_sample_kernel(xbc_ref, z_ref, dt_ref, cs_ref, s0_ref, cw_ref, cb_ref, dtb_ref, a_ref, dsk_ref, gn_ref,
                       y_ref, st_ref, xp_ref, *, l):
    i = pl.program_id(0)
    L = SSD_CHUNK

    @pl.when(i == 0)
    def _():
        xp_ref[...] = jnp.zeros_like(xp_ref)

    xp_ref[5:8, :] = cs_ref[0]
    xp_ref[8:8 + l, :] = xbc_ref[0]
    conv = _conv_from_pad(xp_ref, cw_ref, cb_ref)
    pad = jnp.zeros((L - l, LANE), F32)
    dt = jnp.concatenate([_softplus(dt_ref[0] + dtb_ref[...]), pad], axis=0)
    z = jnp.concatenate([z_ref[0], jnp.zeros((L - l, D_INNER), F32)], axis=0)
    y, new_state = _ssd_chunk(conv, z, dt, s0_ref[0], a_ref[...], dsk_ref[...], gn_ref[...])
    y_ref[0] = y[0:l, :].astype(BF16)
    st_ref[0] = new_state


def _ssd_sample(xbc, z, dt, conv_state, ssm_state, cw, cb, dtb, a_row, dsk, gn):
    b, l, _ = xbc.shape
    L = SSD_CHUNK
    consts = [cw, cb, dtb, a_row, dsk, gn]
    hp = B_HEADS * B_HEAD_DIM
    return pl.pallas_call(
        functools.partial(_ssd_sample_kernel, l=l),
        grid=(b,),
        in_specs=[pl.BlockSpec((1, l, XBC_DIM), lambda i: (i, 0, 0)),
                  pl.BlockSpec((1, l, D_INNER), lambda i: (i, 0, 0)),
                  pl.BlockSpec((1, l, LANE), lambda i: (i, 0, 0)),
                  pl.BlockSpec((1, D_CONV - 1, XBC_DIM), lambda i: (i, 0, 0)),
                  pl.BlockSpec((1, hp, D_STATE), lambda i: (i, 0, 0))] + [_const_spec(c) for c in consts],
        out_specs=[pl.BlockSpec((1, l, D_INNER), lambda i: (i, 0, 0)),
                   pl.BlockSpec((1, hp, D_STATE), lambda i: (i, 0, 0))],
        out_shape=[jax.ShapeDtypeStruct((b, l, D_INNER), BF16),
                   jax.ShapeDtypeStruct((b, hp, D_STATE), F32)],
        scratch_shapes=[pltpu.VMEM((L + 8, XBC_DIM), F32)],
        compiler_params=_params("arbitrary"),
        name="ssd_sample",
    )(xbc, z, dt, conv_state, ssm_state, *consts)


def _proj_res_kernel(*refs, n_in):
    a_refs = refs[:n_in]
    x_ref, g_ref = refs[n_in:n_in + 2]
    w_refs = refs[n_in + 2:2 * n_in + 2]
    o_ref = refs[-1]
    acc = None
    for a_ref, w_ref in zip(a_refs, w_refs):
        part = jnp.dot(a_ref[...], w_ref[...], preferred_element_type=F32)
        acc = part if acc is None else acc + part
    o_ref[...] = x_ref[...] + g_ref[...] * acc


def _proj_res(acts, x, gate, ws):
    t, d = x.shape
    tm = _tile(t, 512)
    n_in = len(acts)
    return pl.pallas_call(
        functools.partial(_proj_res_kernel, n_in=n_in),
        grid=(t // tm,),
        in_specs=[pl.BlockSpec((tm, a.shape[1]), lambda i: (i, 0)) for a in acts]
        + [pl.BlockSpec((tm, d), lambda i: (i, 0)), _row_spec(gate, tm)] + [_const_spec(w) for w in ws],
        out_specs=pl.BlockSpec((tm, d), lambda i: (i, 0)),
        out_shape=jax.ShapeDtypeStruct((t, d), F32),
        compiler_params=_params("arbitrary"),
        name="proj_residual",
    )(*acts, x, gate, *ws)


def _swiglu_kernel(x_ref, sh_ref, sc_ref, g_ref, wgu_ref, wd_ref, o_ref, *, n_chunk):
    x = x_ref[...]
    h = (_rms(x) * (1.0 + sc_ref[...]) + sh_ref[...]).astype(BF16)
    f = wd_ref.shape[0]
    tf = f // n_chunk
    acc = None
    for c in range(n_chunk):
        gp = jnp.dot(h, wgu_ref[:, c * tf:(c + 1) * tf], preferred_element_type=F32)
        up = jnp.dot(h, wgu_ref[:, f + c * tf:f + (c + 1) * tf], preferred_element_type=F32)
        a = (_silu(gp) * up).astype(BF16)
        part = jnp.dot(a, wd_ref[c * tf:(c + 1) * tf, :], preferred_element_type=F32)
        acc = part if acc is None else acc + part
    o_ref[...] = x + g_ref[...] * acc


def _swiglu(x, sh, sc, gate, wgu, wd):
    t, d = x.shape
    tm = _tile(t, 512)
    return pl.pallas_call(
        functools.partial(_swiglu_kernel, n_chunk=2),
        grid=(t // tm,),
        in_specs=[pl.BlockSpec((tm, d), lambda i: (i, 0)), _row_spec(sh, tm), _row_spec(sc, tm), _row_spec(gate, tm),
                  _const_spec(wgu), _const_spec(wd)],
        out_specs=pl.BlockSpec((tm, d), lambda i: (i, 0)),
        out_shape=jax.ShapeDtypeStruct((t, d), F32),
        compiler_params=_params("arbitrary"),
        name="swiglu",
    )(x, sh, sc, gate, wgu, wd)


def _rope_lanes(blk, t_ref):
    half = C_ROPE // 2
    return (blk * t_ref[0] + pltpu.roll(blk, half, 1) * t_ref[1]
            + pltpu.roll(blk, LANE - half, 1) * t_ref[2])


def _od_latents(x_ref, sh_ref, sc_ref, w_in_ref, qn_ref, kvn_ref):
    x = x_ref[...]
    h = (_rms(x) * (1.0 + sc_ref[...]) + sh_ref[...]).astype(BF16)
    y = jnp.dot(h, w_in_ref[...], preferred_element_type=F32)
    cqn = (_rms(y[:, :Q_LORA]) * qn_ref[...]).astype(BF16)
    ckvn = _rms(y[:, Q_LORA:Q_LORA + KV_LORA]) * kvn_ref[...]
    return cqn, ckvn, y[:, Q_LORA + KV_LORA:]


def _od_in_prompt_kernel(x_ref, sh_ref, sc_ref, w_in_ref, qn_ref, kvn_ref, wuq_ref, wuk_ref, wuv_ref,
                         tq_ref, tk_ref, q_out, k_out, v_out, ckv_out, kpe_out):
    cqn, ckvn, kpe_pad = _od_latents(x_ref, sh_ref, sc_ref, w_in_ref, qn_ref, kvn_ref)
    ckv_out[...] = ckvn
    ckb = ckvn.astype(BF16)
    qf = jnp.dot(cqn, wuq_ref[...], preferred_element_type=F32)
    kf = jnp.dot(ckb, wuk_ref[...], preferred_element_type=F32)
    v_out[...] = jnp.dot(ckb, wuv_ref[...], preferred_element_type=F32).astype(BF16)
    kpr = _rope_lanes(kpe_pad, tk_ref)
    kpe_out[...] = kpr[:, C_NOPE:C_NOPE + C_ROPE]
    for hh in range(C_HEADS):
        q_out[hh] = _rope_lanes(qf[:, LANE * hh:LANE * (hh + 1)], tq_ref).astype(BF16)
        k_out[hh] = (kf[:, LANE * hh:LANE * (hh + 1)] + kpr).astype(BF16)


def _od_in_prompt(x, sh, sc, w_in, qn, kvn, wuq, wuk, wuv, tq, tk):
    t, d = x.shape
    tm = _tile(t, 512)
    consts = [w_in, qn, kvn, wuq, wuk, wuv]
    return pl.pallas_call(
        _od_in_prompt_kernel,
        grid=(t // tm,),
        in_specs=[pl.BlockSpec((tm, d), lambda i: (i, 0)), _row_spec(sh, tm), _row_spec(sc, tm)]
        + [_const_spec(c) for c in consts]
        + [pl.BlockSpec((3, tm, LANE), lambda i: (0, i, 0)), pl.BlockSpec((3, tm, LANE), lambda i: (0, i, 0))],
        out_specs=[pl.BlockSpec((C_HEADS, tm, LANE), lambda i: (0, i, 0)),
                   pl.BlockSpec((C_HEADS, tm, LANE), lambda i: (0, i, 0)),
                   pl.BlockSpec((tm, C_HEADS * C_V), lambda i: (i, 0)),
                   pl.BlockSpec((tm, KV_LORA), lambda i: (i, 0)),
                   pl.BlockSpec((tm, C_ROPE), lambda i: (i, 0))],
        out_shape=[jax.ShapeDtypeStruct((C_HEADS, t, LANE), BF16),
                   jax.ShapeDtypeStruct((C_HEADS, t, LANE), BF16),
                   jax.ShapeDtypeStruct((t, C_HEADS * C_V), BF16),
                   jax.ShapeDtypeStruct((t, KV_LORA), F32),
                   jax.ShapeDtypeStruct((t, C_ROPE), F32)],
        compiler_params=_params("arbitrary"),
        name="mla_proj_prompt",
    )(x, sh, sc, *consts, tq, tk)


def _od_in_sample_kernel(x_ref, sh_ref, sc_ref, w_in_ref, qn_ref, kvn_ref, wuq_ref, wukt_ref,
                         tq_ref, tk_ref, ql_out, qp_out, ckv_out, kpe_out, kpp_out):
    cqn, ckvn, kpe_pad = _od_latents(x_ref, sh_ref, sc_ref, w_in_ref, qn_ref, kvn_ref)
    ckv_out[...] = ckvn
    kpr = _rope_lanes(kpe_pad, tk_ref)
    kpe_out[...] = kpr[:, 0:C_ROPE]
    kpp_out[...] = kpr
    qf = jnp.dot(cqn, wuq_ref[...], preferred_element_type=F32)
    nq = C_HEADS * LANE
    for hh in range(C_HEADS):
        qn_h = (qf[:, LANE * hh:LANE * (hh + 1)] * MLA_SCALE).astype(BF16)
        ql = jnp.dot(qn_h, wukt_ref[hh], preferred_element_type=F32)
        ql_out[:, KV_LORA * hh:KV_LORA * (hh + 1)] = ql.astype(BF16)
        qp_out[:, LANE * hh:LANE * (hh + 1)] = _rope_lanes(qf[:, nq + LANE * hh:nq + LANE * (hh + 1)], tq_ref).astype(BF16)


def _od_in_sample(x, sh, sc, w_in, qn, kvn, wuq, wukt, tq, tk):
    t, d = x.shape
    tm = _tile(t, 512)
    consts = [w_in, qn, kvn, wuq, wukt]
    widths = [(C_HEADS * KV_LORA, BF16), (C_HEADS * LANE, BF16), (KV_LORA, F32), (C_ROPE, F32), (LANE, F32)]
    return pl.pallas_call(
        _od_in_sample_kernel,
        grid=(t // tm,),
        in_specs=[pl.BlockSpec((tm, d), lambda i: (i, 0)), _row_spec(sh, tm), _row_spec(sc, tm)]
        + [_const_spec(c) for c in consts]
        + [pl.BlockSpec((3, tm, LANE), lambda i: (0, i, 0)), pl.BlockSpec((3, tm, LANE), lambda i: (0, i, 0))],
        out_specs=[pl.BlockSpec((tm, n), lambda i: (i, 0)) for n, _ in widths],
        out_shape=[jax.ShapeDtypeStruct((t, n), dt) for n, dt in widths],
        compiler_params=_params("arbitrary"),
        name="mla_proj_sample",
    )(x, sh, sc, *consts, tq, tk)


def _flash_kernel(q_ref, k_ref, v_ref, o_ref, m_ref, l_ref, acc_ref, *, tq, tk):
    qi = pl.program_id(1)
    m_ref[...] = jnp.full(m_ref.shape, -jnp.inf, F32)
    l_ref[...] = jnp.zeros(l_ref.shape, F32)
    acc_ref[...] = jnp.zeros(acc_ref.shape, F32)

    def chunk(start, diag_off):
        v = v_ref[pl.ds(start, tk), :]
        for a in range(2):
            s = lax.dot_general(q_ref[a], k_ref[a, pl.ds(start, tk), :], _NT, preferred_element_type=F32)
            if diag_off is not None:
                row = lax.broadcasted_iota(jnp.int32, (tq, tk), 0)
                col = lax.broadcasted_iota(jnp.int32, (tq, tk), 1) + diag_off
                s = jnp.where(col <= row, s, -jnp.inf)
            m_prev = m_ref[a]
            m_new = jnp.maximum(m_prev, jnp.max(s, axis=-1, keepdims=True))
            alpha = jnp.exp(m_prev - m_new)
            p = jnp.exp(s - m_new)
            l_ref[a] = alpha * l_ref[a] + jnp.sum(p, axis=-1, keepdims=True)
            acc_ref[a] = alpha * acc_ref[a] + jnp.dot(p.astype(BF16), v, preferred_element_type=F32)
            m_ref[a] = m_new

    def body(j, carry):
        chunk(pl.multiple_of(j * tk, tk), None)
        return carry

    lax.fori_loop(0, qi * (tq // tk), body, 0)
    for dd in range(tq // tk):
        chunk(pl.multiple_of(qi * tq + dd * tk, tk), dd * tk)
    lane = lax.broadcasted_iota(jnp.int32, (tq, LANE), 1)
    o = jnp.where(lane < C_V, acc_ref[0] / l_ref[0], acc_ref[1] / l_ref[1])
    o_ref[...] = o.astype(BF16)


def _flash_prompt(qh, kh, v):
    t = v.shape[0]
    tq = _tile(t, 1024)
    tk = _tile(tq, 512)
    return pl.pallas_call(
        functools.partial(_flash_kernel, tq=tq, tk=tk),
        grid=(C_HEADS // 2, t // tq),
        in_specs=[pl.BlockSpec((2, tq, LANE), lambda hp, qi: (hp, qi, 0)),
                  pl.BlockSpec((2, t, LANE), lambda hp, qi: (hp, 0, 0)),
                  pl.BlockSpec((t, LANE), lambda hp, qi: (0, hp))],
        out_specs=pl.BlockSpec((tq, LANE), lambda hp, qi: (qi, hp)),
        out_shape=jax.ShapeDtypeStruct((t, C_HEADS * C_V), BF16),
        scratch_shapes=[pltpu.VMEM((2, tq, 1), F32), pltpu.VMEM((2, tq, 1), F32), pltpu.VMEM((2, tq, LANE), F32)],
        compiler_params=_params("arbitrary", "arbitrary"),
        name="mla_flash_prompt",
    )(qh, kh, v)


def _paged_kernel(pt_ref, ql_ref, qp_ref, cn_ref, kn_ref, *refs, npg, l):
    del pt_ref
    ck_refs = refs[:npg]
    kp_refs = refs[npg:2 * npg]
    o_ref = refs[2 * npg]
    m_ref, l_ref, acc_ref = refs[2 * npg + 1:]
    s_idx = pl.program_id(1)
    nq = ql_ref.shape[1]

    @pl.when(s_idx == 0)
    def _():
        m_ref[...] = jnp.full(m_ref.shape, -jnp.inf, F32)
        l_ref[...] = jnp.zeros(l_ref.shape, F32)
        acc_ref[...] = jnp.zeros(acc_ref.shape, F32)

    ql = ql_ref[0]
    qp4 = qp_ref[0]
    lane = lax.broadcasted_iota(jnp.int32, (nq, LANE), 1)

    def col_of(row_vec):
        return jnp.broadcast_to(row_vec, (LANE, nq)).T

    def update(s_list, ck_list):
        mx = s_list[0].max(axis=0, keepdims=True)
        for s in s_list[1:]:
            mx = jnp.maximum(mx, s.max(axis=0, keepdims=True))
        m_prev = m_ref[...]
        m_new = jnp.maximum(m_prev, mx)
        alpha = jnp.exp(m_prev - m_new)
        psum = None
        pv = None
        for s, ck in zip(s_list, ck_list):
            p = jnp.exp(s - m_new)
            ps = p.sum(axis=0, keepdims=True)
            psum = ps if psum is None else psum + ps
            part = jnp.dot(p.T.astype(BF16), ck, preferred_element_type=F32)
            pv = part if pv is None else pv + part
        l_ref[...] = alpha * l_ref[...] + psum
        a_col = col_of(alpha)
        acc_ref[...] = jnp.concatenate([a_col, a_col], axis=1) * acc_ref[...] + pv
        m_ref[...] = m_new

    ck = jnp.concatenate([r[0] for r in ck_refs], axis=0)
    kp = jnp.concatenate([r[0] for r in kp_refs], axis=0).astype(BF16)
    s_list, ck_list = [], []
    for j in range(4):
        ckj = ck[:, KV_LORA * j:KV_LORA * (j + 1)].astype(BF16)
        qpj = jnp.where((lane >= C_ROPE * j) & (lane < C_ROPE * (j + 1)), qp4, jnp.zeros_like(qp4))
        s = (lax.dot_general(ckj, ql, _NT, preferred_element_type=F32)
             + lax.dot_general(kp, qpj, _NT, preferred_element_type=F32))
        s_list.append(s)
        ck_list.append(ckj)
    update(s_list, ck_list)

    @pl.when(s_idx == pl.num_programs(1) - 1)
    def _():
        cn = jnp.concatenate([cn_ref[0], jnp.zeros((LANE - l, KV_LORA), F32)], axis=0).astype(BF16)
        kn = jnp.concatenate([kn_ref[0], jnp.zeros((LANE - l, LANE), F32)], axis=0).astype(BF16)
        qp0 = jnp.where(lane < C_ROPE, qp4, jnp.zeros_like(qp4))
        s = (lax.dot_general(cn, ql, _NT, preferred_element_type=F32)
             + lax.dot_general(kn, qp0, _NT, preferred_element_type=F32))
        krow = lax.broadcasted_iota(jnp.int32, (LANE, nq), 0)
        qtok = lax.broadcasted_iota(jnp.int32, (LANE, nq), 1) // C_HEADS
        s = jnp.where(krow <= qtok, s, -jnp.inf)
        update([s], [cn])
        l_col = col_of(l_ref[...])
        o_ref[0] = (acc_ref[...] / jnp.concatenate([l_col, l_col], axis=1)).astype(BF16)


def _paged_attention(page_table, ql, qp4, ckv_new, kpe_new_pad, cache_ckv, cache_kpe, npg):
    b, nq, _ = ql.shape
    l = ckv_new.shape[1]
    n_pages = page_table.shape[1]
    rows = PAGE_SIZE * KV_LORA // 1024

    def page_spec(width, k):
        return pl.BlockSpec((1, rows, width), lambda bi, si, pt, k=k: (pt[bi, si * npg + k], 0, 0))

    grid_spec = pltpu.PrefetchScalarGridSpec(
        num_scalar_prefetch=1,
        grid=(b, n_pages // npg),
        in_specs=[pl.BlockSpec((1, nq, KV_LORA), lambda bi, si, pt: (bi, 0, 0)),
                  pl.BlockSpec((1, nq, LANE), lambda bi, si, pt: (bi, 0, 0)),
                  pl.BlockSpec((1, l, KV_LORA), lambda bi, si, pt: (bi, 0, 0)),
                  pl.BlockSpec((1, l, LANE), lambda bi, si, pt: (bi, 0, 0))]
        + [page_spec(1024, k) for k in range(npg)] + [page_spec(LANE, k) for k in range(npg)],
        out_specs=pl.BlockSpec((1, nq, KV_LORA), lambda bi, si, pt: (bi, 0, 0)),
        scratch_shapes=[pltpu.VMEM((1, nq), F32), pltpu.VMEM((1, nq), F32), pltpu.VMEM((nq, KV_LORA), F32)],
    )
    return pl.pallas_call(
        functools.partial(_paged_kernel, npg=npg, l=l),
        grid_spec=grid_spec,
        out_shape=jax.ShapeDtypeStruct((b, nq, KV_LORA), BF16),
        compiler_params=_params("arbitrary", "arbitrary"),
        name="mla_paged_sample",
    )(page_table, ql, qp4, ckv_new, kpe_new_pad, *([cache_ckv] * npg), *([cache_kpe] * npg))


def _mla_out_sample_kernel(o_ref, x_ref, g_ref, wuv_ref, wo_ref, out_ref):
    parts = []
    for pr in range(C_HEADS // 2):
        parts.append(jnp.dot(o_ref[:, 2 * KV_LORA * pr:2 * KV_LORA * (pr + 1)], wuv_ref[pr],
                             preferred_element_type=F32))
    attn = jnp.concatenate(parts, axis=-1).astype(BF16)
    out_ref[...] = x_ref[...] + g_ref[...] * jnp.dot(attn, wo_ref[...], preferred_element_type=F32)


def _mla_out_sample(o_lat, x, gate, wuv_pairs, wo):
    t, d = x.shape
    tm = _tile(t, 512)
    return pl.pallas_call(
        _mla_out_sample_kernel,
        grid=(t // tm,),
        in_specs=[pl.BlockSpec((tm, o_lat.shape[1]), lambda i: (i, 0)), pl.BlockSpec((tm, d), lambda i: (i, 0)),
                  _row_spec(gate, tm), _const_spec(wuv_pairs), _const_spec(wo)],
        out_specs=pl.BlockSpec((tm, d), lambda i: (i, 0)),
        out_shape=jax.ShapeDtypeStruct((t, d), F32),
        compiler_params=_params("arbitrary"),
        name="mla_out_sample",
    )(o_lat, x, gate, wuv_pairs, wo)


def _route_kernel(x_ref, sh_ref, sc_ref, r_ref, h_ref, gate_ref):
    x = x_ref[...]
    h = _rms(x) * (1.0 + sc_ref[...]) + sh_ref[...]
    h_ref[...] = h.astype(BF16)
    logits = jnp.dot(h, r_ref[...], precision=_HI, preferred_element_type=F32)
    lane = lax.broadcasted_iota(jnp.int32, logits.shape, 1)
    logits = jnp.where(lane < N_EXPERTS, logits, -jnp.inf)
    v1 = jnp.max(logits, axis=-1, keepdims=True)
    i1 = jnp.min(jnp.where(logits == v1, lane, LANE), axis=-1, keepdims=True)
    rest = jnp.where(lane == i1, -jnp.inf, logits)
    v2 = jnp.max(rest, axis=-1, keepdims=True)
    i2 = jnp.min(jnp.where(rest == v2, lane, LANE), axis=-1, keepdims=True)
    e2 = jnp.exp(v2 - v1)
    g1 = 1.0 / (1.0 + e2)
    g2 = e2 / (1.0 + e2)
    gate_ref[...] = jnp.where(lane == i1, g1, 0.0) + jnp.where(lane == i2, g2, 0.0)


def _route(x, sh, sc, router_pad):
    t, d = x.shape
    tm = _tile(t, 512)
    return pl.pallas_call(
        _route_kernel,
        grid=(t // tm,),
        in_specs=[pl.BlockSpec((tm, d), lambda i: (i, 0)), _row_spec(sh, tm), _row_spec(sc, tm), _const_spec(router_pad)],
        out_specs=[pl.BlockSpec((tm, d), lambda i: (i, 0)), pl.BlockSpec((tm, LANE), lambda i: (i, 0))],
        out_shape=[jax.ShapeDtypeStruct((t, d), BF16), jax.ShapeDtypeStruct((t, LANE), F32)],
        compiler_params=_params("arbitrary"),
        name="moe_route",
    )(x, sh, sc, router_pad)


def _moe_dense_kernel(h_ref, gate_ref, x_ref, g2_ref, fn_ref, wg_ref, wu_ref, wd_ref, o_ref, acc_ref):
    e = pl.program_id(1)
    c = pl.program_id(2)

    @pl.when((e == 0) & (c == 0))
    def _():
        acc_ref[...] = jnp.zeros_like(acc_ref)

    h = h_ref[...]
    gate = gate_ref[...]
    lane = lax.broadcasted_iota(jnp.int32, gate.shape, 1)
    gcol = jnp.sum(jnp.where(lane == e, gate, 0.0), axis=-1, keepdims=True)
    gp = jnp.dot(h, wg_ref[0], preferred_element_type=F32)
    up = jnp.dot(h, wu_ref[0], preferred_element_type=F32)
    a = (_silu(gp) * up * gcol).astype(BF16)
    acc_ref[...] += jnp.dot(a, wd_ref[0], preferred_element_type=F32)

    @pl.when((e == pl.num_programs(1) - 1) & (c == pl.num_programs(2) - 1))
    def _():
        xo = x_ref[...] + g2_ref[...] * acc_ref[...]
        o_ref[...] = _rms(xo) * fn_ref[...]


def _moe_dense(h, gate, x, g2, fn, wgu, wd):
    t, d = x.shape
    ne, f, _ = wd.shape
    tm = _tile(t, 512)
    tf = 512
    nc = f // tf
    return pl.pallas_call(
        _moe_dense_kernel,
        grid=(t // tm, ne, nc),
        in_specs=[pl.BlockSpec((tm, d), lambda i, e, c: (i, 0)),
                  pl.BlockSpec((tm, LANE), lambda i, e, c: (i, 0)),
                  pl.BlockSpec((tm, d), lambda i, e, c: (i, 0)),
                  (pl.BlockSpec((1, d), lambda i, e, c: (0, 0)) if g2.shape[0] == 1
                   else pl.BlockSpec((tm, d), lambda i, e, c: (i, 0))),
                  pl.BlockSpec((1, d), lambda i, e, c: (0, 0)),
                  pl.BlockSpec((1, d, tf), lambda i, e, c: (e, 0, c)),
                  pl.BlockSpec((1, d, tf), lambda i, e, c: (e, 0, nc + c)),
                  pl.BlockSpec((1, tf, d), lambda i, e, c: (e, c, 0))],
        out_specs=pl.BlockSpec((tm, d), lambda i, e, c: (i, 0)),
        out_shape=jax.ShapeDtypeStruct((t, d), F32),
        scratch_shapes=[pltpu.VMEM((tm, d), F32)],
        compiler_params=_params("arbitrary", "arbitrary", "arbitrary"),
        name="moe_dense",
    )(h, gate, x, g2, fn, wgu, wgu, wd)


def _rope_tables(pos, lo, reps, scale):
    half = C_ROPE // 2
    freqs = ROPE_THETA ** (-2.0 * jnp.arange(half, dtype=F32) / C_ROPE)
    ang = pos.astype(F32)[:, None] * freqs[None]
    cos, sin = jnp.cos(ang), jnp.sin(ang)
    n = pos.shape[0]
    zeros = jnp.zeros((n, half), F32)
    c_grp = jnp.concatenate([cos, cos], axis=1)
    s1_grp = jnp.concatenate([zeros, sin], axis=1)
    s2_grp = jnp.concatenate([-sin, zeros], axis=1)

    def lay(grp, fill):
        body = jnp.tile(grp, (1, reps))
        left = jnp.full((n, lo), fill, F32)
        right = jnp.zeros((n, LANE - lo - reps * C_ROPE), F32)
        return jnp.concatenate([left, body, right], axis=1)

    return jnp.stack([lay(c_grp, 1.0), lay(s1_grp, 0.0), lay(s2_grp, 0.0)]) * scale


def _pad_cols(w, n):
    return jnp.pad(w, ((0, 0), (0, n - w.shape[1])))


def kernel(x_prompt, x_sample, c_prompt, c_sample, state_swa_kv, state_conv, state_ssm, cache_ckv, cache_kpe, page_table,
           ev_mod_w, ev_mod_b, ev_w_in, ev_sinks, ev_conv_w, ev_conv_b, ev_dt_bias, ev_a_log, ev_d_skip, ev_gnorm,
           ev_w_out, ev_w_gu, ev_w_down, od_mod_w, od_mod_b, od_w_in, od_qnorm, od_kvnorm, od_w_uq, od_w_uk, od_w_uv,
           od_w_out, od_router, od_w_gu, od_w_down, final_norm):
    d = D_MODEL
    _, tp, _ = x_prompt.shape
    bs, ls, _ = x_sample.shape
    ts = bs * ls
    n_pages = page_table.shape[1]
    past = n_pages * PAGE_SIZE
    assert state_swa_kv.shape[2] == WINDOW and ls <= 8 and ls >= D_CONV - 1

    xp = x_prompt.reshape(tp, d)
    xs = x_sample.reshape(ts, d)

    n_c = 1 + bs
    n_cp = -(-n_c // 8) * 8
    c_all = jnp.pad(jnp.concatenate([c_prompt, c_sample], axis=0), ((0, n_cp - n_c), (0, 0)))

    def mods(w, b):
        m = _ada_mod(c_all, w, b)
        mp = [m[0:1, k * d:(k + 1) * d] for k in range(MOD_SLOTS)]
        ms = [jnp.repeat(m[1:n_c, k * d:(k + 1) * d], ls, axis=0) for k in range(MOD_SLOTS)]
        return mp, ms

    i = 0
    mp, ms = mods(ev_mod_w[i], ev_mod_b[i])
    w_in = ev_w_in[i]
    o_k, o_v, o_z, o_x, o_dt = A_QW, A_QW + A_KVW, A_QW + 2 * A_KVW, A_QW + 2 * A_KVW + D_INNER, A_QW + 2 * A_KVW + D_INNER + XBC_DIM
    w_in_p = jnp.concatenate([w_in[:, :o_k], w_in[:, o_z:o_x], w_in[:, o_x:o_dt], w_in[:, o_k:o_z],
                              _pad_cols(w_in[:, o_dt:], LANE)], axis=1).astype(BF16)
    sinks = ev_sinks[i].astype(F32)
    cw = ev_conv_w[i]
    cb = ev_conv_b[i].reshape(1, XBC_DIM)
    dtb = _pad_cols(ev_dt_bias[i].reshape(1, B_HEADS).astype(F32), LANE)
    a_row = _pad_cols(-jnp.exp(ev_a_log[i].astype(F32)).reshape(1, B_HEADS), LANE)
    dsk = jnp.repeat(ev_d_skip[i].astype(F32), B_HEAD_DIM).reshape(1, D_INNER)
    gn = ev_gnorm[i].reshape(1, D_INNER)
    w_out = ev_w_out[i].astype(BF16)
    w_gu = ev_w_gu[i].astype(BF16)
    w_dn = ev_w_down[i].astype(BF16)
    ssd_consts = (cw, cb, dtb, a_row, dsk, gn)

    q, z, xbc, kv, dt = _ev_in(xp, mp[0], mp[1], w_in_p)
    attn = _swa_prompt(q, kv, sinks)
    ssm, ssm_state_p = _ssd_prompt(xbc, z, dt, *ssd_consts)
    xp = _proj_res([attn, ssm], xp, mp[2], [w_out[:A_QW], w_out[A_QW:]])
    xp = _swiglu(xp, mp[3], mp[4], mp[5], w_gu, w_dn)
    swa_kv_prompt = kv[tp - WINDOW:].reshape(1, 1, WINDOW, 2, A_KV_HEADS, A_HEAD_DIM)
    conv_prompt = xbc[tp - (D_CONV - 1):].reshape(1, 1, D_CONV - 1, XBC_DIM)
    ssm_prompt = ssm_state_p.reshape(1, 1, B_HEADS, B_HEAD_DIM, D_STATE)

    q, z, xbc, kv, dt = _ev_in(xs, ms[0], ms[1], w_in_p)
    buf = state_swa_kv[i].reshape(bs, WINDOW, 2 * A_KVW)
    attn, nbuf = _swa_sample(q.reshape(bs, ls, A_QW), kv.reshape(bs, ls, 2 * A_KVW), buf, sinks)
    xbc3 = xbc.reshape(bs, ls, XBC_DIM)
    ssm, ssm_state_s = _ssd_sample(xbc3, z.reshape(bs, ls, D_INNER), dt.reshape(bs, ls, LANE), state_conv[i],
                                   state_ssm[i].reshape(bs, B_HEADS * B_HEAD_DIM, D_STATE), *ssd_consts)
    xs = _proj_res([attn.reshape(ts, A_QW), ssm.reshape(ts, D_INNER)], xs, ms[2], [w_out[:A_QW], w_out[A_QW:]])
    xs = _swiglu(xs, ms[3], ms[4], ms[5], w_gu, w_dn)
    swa_kv_sample = nbuf.reshape(1, bs, WINDOW, 2, A_KV_HEADS, A_HEAD_DIM)
    conv_sample = xbc3[:, ls - (D_CONV - 1):].reshape(1, bs, D_CONV - 1, XBC_DIM)
    ssm_sample = ssm_state_s.reshape(1, bs, B_HEADS, B_HEAD_DIM, D_STATE)

    mp, ms = mods(od_mod_w[i], od_mod_b[i])
    w_in = od_w_in[i]
    w_cq, w_ckv, w_kpe = w_in[:, :Q_LORA], w_in[:, Q_LORA:Q_LORA + KV_LORA], w_in[:, Q_LORA + KV_LORA:]
    qn = od_qnorm[i].reshape(1, Q_LORA)
    kvn = od_kvnorm[i].reshape(1, KV_LORA)
    w_uq = od_w_uq[i].reshape(Q_LORA, C_HEADS, C_NOPE + C_ROPE)
    w_uk = od_w_uk[i]
    w_uv = od_w_uv[i]
    w_o = od_w_out[i].astype(BF16)

    w_in_pp = jnp.concatenate([w_cq, w_ckv, jnp.pad(w_kpe, ((0, 0), (C_NOPE, LANE - C_NOPE - C_ROPE)))], axis=1).astype(BF16)
    wuq_p = jnp.pad(w_uq, ((0, 0), (0, 0), (0, LANE - C_NOPE - C_ROPE))).reshape(Q_LORA, C_HEADS * LANE).astype(BF16)
    wuk_p = jnp.pad(w_uk, ((0, 0), (0, 0), (0, LANE - C_NOPE))).reshape(KV_LORA, C_HEADS * LANE).astype(BF16)
    wuv_p = w_uv.reshape(KV_LORA, C_HEADS * C_V).astype(BF16)
    pos_p = jnp.arange(tp, dtype=jnp.int32)
    tq_p = _rope_tables(pos_p, C_NOPE, 1, MLA_SCALE)
    tk_p = _rope_tables(pos_p, C_NOPE, 1, 1.0)
    qh, kh, v, ckv_p, kpe_p = _od_in_prompt(xp, mp[0], mp[1], w_in_pp, qn, kvn, wuq_p, wuk_p, wuv_p, tq_p, tk_p)
    attn = _flash_prompt(qh, kh, v)
    xp = _proj_res([attn], xp, mp[2], [w_o])

    w_in_ps = jnp.concatenate([w_cq, w_ckv, _pad_cols(w_kpe, LANE)], axis=1).astype(BF16)
    wq_nope = jnp.pad(w_uq[:, :, :C_NOPE], ((0, 0), (0, 0), (0, LANE - C_NOPE))).reshape(Q_LORA, C_HEADS * LANE)
    wq_rope = jnp.tile(w_uq[:, :, C_NOPE:], (1, 1, LANE // C_ROPE)).reshape(Q_LORA, C_HEADS * LANE)
    wuq_s = jnp.concatenate([wq_nope, wq_rope], axis=1).astype(BF16)
    wukt_s = jnp.pad(jnp.transpose(w_uk, (1, 2, 0)), ((0, 0), (0, LANE - C_NOPE), (0, 0))).astype(BF16)
    pos_s = jnp.tile(past + jnp.arange(ls, dtype=jnp.int32), bs)
    tq_s = _rope_tables(pos_s, 0, LANE // C_ROPE, MLA_SCALE)
    tk_s = _rope_tables(pos_s, 0, 1, 1.0)
    ql, qp4, ckv_s, kpe_s, kpe_pad_s = _od_in_sample(xs, ms[0], ms[1], w_in_ps, qn, kvn, wuq_s, wukt_s, tq_s, tk_s)
    nq = ls * C_HEADS
    n_pool = cache_ckv.shape[1]
    rows = PAGE_SIZE * KV_LORA // 1024
    npg = 8 if n_pages % 8 == 0 else n_pages
    o_lat = _paged_attention(page_table, ql.reshape(bs, nq, KV_LORA), qp4.reshape(bs, nq, LANE),
                             ckv_s.reshape(bs, ls, KV_LORA), kpe_pad_s.reshape(bs, ls, LANE),
                             cache_ckv[i].reshape(n_pool, rows, 1024), cache_kpe[i].reshape(n_pool, rows, LANE), npg)
    wuv_h = jnp.transpose(w_uv, (1, 0, 2)).reshape(C_HEADS // 2, 2, KV_LORA, C_V)
    zero = jnp.zeros((C_HEADS // 2, KV_LORA, C_V), F32)
    wuv_pairs = jnp.concatenate([jnp.concatenate([wuv_h[:, 0], zero], axis=2),
                                 jnp.concatenate([zero, wuv_h[:, 1]], axis=2)], axis=1).astype(BF16)
    xs = _mla_out_sample(o_lat.reshape(ts, C_HEADS * KV_LORA), xs, ms[2], wuv_pairs, w_o)

    router_pad = _pad_cols(od_router[i].astype(F32), LANE)
    wgu_e = od_w_gu[i].astype(BF16)
    wdn_e = od_w_down[i].astype(BF16)
    fn = final_norm.reshape(1, d).astype(F32)
    h, gate = _route(xp, mp[3], mp[4], router_pad)
    y_prompt = _moe_dense(h, gate, xp, mp[5], fn, wgu_e, wdn_e)
    h, gate = _route(xs, ms[3], ms[4], router_pad)
    y_sample = _moe_dense(h, gate, xs, ms[5], fn, wgu_e, wdn_e)

    return (y_prompt.reshape(1, tp, d), y_sample.reshape(bs, ls, d),
            swa_kv_prompt, swa_kv_sample, conv_prompt, conv_sample, ssm_prompt, ssm_sample,
            ckv_p.reshape(1, 1, tp, KV_LORA), ckv_s.reshape(1, bs, ls, KV_LORA),
            kpe_p.reshape(1, 1, tp, C_ROPE), kpe_s.reshape(1, bs, ls, C_ROPE))
```

```python
import functools
import math

import jax
import jax.numpy as jnp
from jax import lax
from jax.experimental import pallas as pl
from jax.experimental.pallas import tpu as pltpu

F32 = jnp.float32
BF16 = jnp.bfloat16

D_MODEL = 1024
EPS = 1e-6
MOD_SLOTS = 6

A_HEAD_DIM = 64
A_HEADS = 8
A_KV_HEADS = 2
A_REP = A_HEADS // A_KV_HEADS
WINDOW = 128
A_QW = A_HEADS * A_HEAD_DIM
A_KVW = A_KV_HEADS * A_HEAD_DIM

D_INNER = 512
B_HEAD_DIM = 64
B_HEADS = 8
B_GROUPS = 2
D_STATE = 128
D_CONV = 4
SSD_CHUNK = 128
XBC_DIM = D_INNER + 2 * B_GROUPS * D_STATE

C_HEADS = 16
C_NOPE = 64
C_ROPE = 32
C_V = 64
Q_LORA = 384
KV_LORA = 256
ROPE_THETA = 10000.0
MLA_SCALE = (C_NOPE + C_ROPE) ** -0.5
PAGE_SIZE = 128

D_FF = 2816
N_EXPERTS = 8
D_FF_EXPERT = 3584

LANE = 128
MLA_TK = 512
MLA_TQ = 512
LOG2E = 1.4426950408889634
PAGED_STREAMS = 1
VMEM_LIMIT = 56 * 1024 * 1024

_HI = lax.Precision.HIGHEST
_NT = (((1,), (1,)), ((), ()))
_TN = (((0,), (0,)), ((), ()))


def _params(*sem):
    return pltpu.CompilerParams(dimension_semantics=sem, vmem_limit_bytes=VMEM_LIMIT)


def _tile(n, pref):
    t = min(n, pref)
    while n % t:
        t -= 8
    return t


def _rms(x):
    return x * lax.rsqrt(jnp.mean(x * x, axis=-1, keepdims=True) + EPS)


def _silu(x):
    return x * jax.nn.sigmoid(x)


def _softplus(x):
    e = jnp.exp(-jnp.abs(x))
    u = 1.0 + e
    lg = jnp.where(u == 1.0, e, jnp.log(u) * e / (u - 1.0))
    return jnp.maximum(x, 0.0) + lg


def _row_spec(arr, tm):
    d = arr.shape[1]
    if arr.shape[0] == 1:
        return pl.BlockSpec((1, d), lambda i: (0, 0))
    return pl.BlockSpec((tm, d), lambda i: (i, 0))


def _const_spec(arr):
    nd = arr.ndim
    return pl.BlockSpec(arr.shape, lambda *_: (0,) * nd)


def _mod_kernel(c_ref, w_ref, b_ref, o_ref):
    s = _silu(c_ref[...]).astype(BF16)
    o_ref[...] = jnp.dot(s, w_ref[...].astype(BF16), preferred_element_type=F32) + b_ref[...]


def _ada_mod(c_all, w, b):
    r, d = c_all.shape
    n = w.shape[1]
    tn = 1024
    return pl.pallas_call(
        _mod_kernel,
        grid=(n // tn,),
        in_specs=[pl.BlockSpec((r, d), lambda j: (0, 0)),
                  pl.BlockSpec((d, tn), lambda j: (0, j)),
                  pl.BlockSpec((1, tn), lambda j: (0, j))],
        out_specs=pl.BlockSpec((r, tn), lambda j: (0, j)),
        out_shape=jax.ShapeDtypeStruct((r, n), F32),
        compiler_params=_params("arbitrary"),
        name="ada_mod",
    )(c_all, w, b.reshape(1, n))


def _ev_in_kernel(x_ref, sh_ref, sc_ref, w_ref, q_ref, z_ref, xbc_ref, kv_ref, dt_ref):
    x = x_ref[...]
    h = _rms(x) * (1.0 + sc_ref[...]) + sh_ref[...]
    y = jnp.dot(h.astype(BF16), w_ref[...], preferred_element_type=F32)
    q_ref[...] = y[:, 0:A_QW].astype(BF16)
    z_ref[...] = y[:, A_QW:A_QW + D_INNER]
    xbc_ref[...] = y[:, 1024:1024 + XBC_DIM]
    kv_ref[...] = y[:, 2048:2048 + 2 * A_KVW]
    dt_ref[...] = y[:, 2304:2304 + LANE]


def _ev_in(x, sh, sc, w):
    t, d = x.shape
    tm = _tile(t, 512)
    outs = [(A_QW, BF16), (D_INNER, F32), (XBC_DIM, F32), (2 * A_KVW, F32), (LANE, F32)]
    return pl.pallas_call(
        _ev_in_kernel,
        grid=(t // tm,),
        in_specs=[pl.BlockSpec((tm, d), lambda i: (i, 0)), _row_spec(sh, tm), _row_spec(sc, tm), _const_spec(w)],
        out_specs=[pl.BlockSpec((tm, n), lambda i: (i, 0)) for n, _ in outs],
        out_shape=[jax.ShapeDtypeStruct((t, n), dt) for n, dt in outs],
        compiler_params=_params("arbitrary"),
        name="ev_in_proj",
    )(x, sh, sc, w)


def _swa_heads(q, k_all, v_all, valid, distf, sink_ref):
    outs = []
    for g in range(A_KV_HEADS):
        k_g = k_all[:, A_HEAD_DIM * g:A_HEAD_DIM * (g + 1)].astype(BF16)
        v_g = v_all[:, A_HEAD_DIM * g:A_HEAD_DIM * (g + 1)].astype(BF16)
        for r in range(A_REP):
            h = g * A_REP + r
            q_h = q[:, A_HEAD_DIM * h:A_HEAD_DIM * (h + 1)]
            s = lax.dot_general(q_h, k_g, _NT, preferred_element_type=F32) * (A_HEAD_DIM ** -0.5)
            s = s - (2.0 ** (-8.0 * (h + 1) / A_HEADS)) * distf
            s = jnp.where(valid, s, -jnp.inf)
            sink = sink_ref[h]
            m = jnp.maximum(jnp.max(s, axis=-1, keepdims=True), sink)
            p = jnp.exp(s - m)
            den = jnp.sum(p, axis=-1, keepdims=True) + jnp.exp(sink - m)
            outs.append(jnp.dot(p.astype(BF16), v_g, preferred_element_type=F32) / den)
    return jnp.concatenate(outs, axis=-1)


def _swa_prompt_kernel(sink_ref, q_ref, kvp_ref, kvc_ref, o_ref):
    i = pl.program_id(0)
    kv = jnp.concatenate([kvp_ref[...], kvc_ref[...]], axis=0)
    row = lax.broadcasted_iota(jnp.int32, (WINDOW, 2 * WINDOW), 0)
    col = lax.broadcasted_iota(jnp.int32, (WINDOW, 2 * WINDOW), 1)
    dist = row + WINDOW - col
    first_key = jnp.where(i > 0, 0, WINDOW)
    valid = (dist >= 0) & (dist < WINDOW) & (col >= first_key)
    o = _swa_heads(q_ref[...], kv[:, :A_KVW], kv[:, A_KVW:], valid, dist.astype(F32), sink_ref)
    o_ref[...] = o.astype(BF16)


def _swa_prompt(q, kv, sinks):
    t = q.shape[0]
    nb = t // WINDOW
    return pl.pallas_call(
        _swa_prompt_kernel,
        grid=(nb,),
        in_specs=[pl.BlockSpec(memory_space=pltpu.SMEM),
                  pl.BlockSpec((WINDOW, A_QW), lambda i: (i, 0)),
                  pl.BlockSpec((WINDOW, 2 * A_KVW), lambda i: (jnp.maximum(i - 1, 0), 0)),
                  pl.BlockSpec((WINDOW, 2 * A_KVW), lambda i: (i, 0))],
        out_specs=pl.BlockSpec((WINDOW, A_QW), lambda i: (i, 0)),
        out_shape=jax.ShapeDtypeStruct((t, A_QW), BF16),
        compiler_params=_params("arbitrary"),
        name="swa_prompt",
    )(sinks, q, kv, kv)


def _swa_sample_kernel(sink_ref, q_ref, kvn_ref, buf_ref, o_ref, nbuf_ref, *, bb, l):
    w = WINDOW
    row = lax.broadcasted_iota(jnp.int32, (l, 2 * w), 0)
    col = lax.broadcasted_iota(jnp.int32, (l, 2 * w), 1)
    dist = row + w - col
    valid = (dist >= 0) & (dist < w)
    distf = dist.astype(F32)
    for b in range(bb):
        buf = buf_ref[b]
        kvn = kvn_ref[b]
        kv = jnp.concatenate([buf, kvn, jnp.zeros((w - l, 2 * A_KVW), F32)], axis=0)
        o = _swa_heads(q_ref[b], kv[:, :A_KVW], kv[:, A_KVW:], valid, distf, sink_ref)
        o_ref[b] = o.astype(BF16)
        nbuf_ref[b, 0:w - l, :] = buf[l:, :]
        nbuf_ref[b, w - l:w, :] = kvn


def _swa_sample(q, kvn, buf, sinks):
    b, l, _ = q.shape
    bb = _tile(b, 8)
    return pl.pallas_call(
        functools.partial(_swa_sample_kernel, bb=bb, l=l),
        grid=(b // bb,),
        in_specs=[pl.BlockSpec(memory_space=pltpu.SMEM),
                  pl.BlockSpec((bb, l, A_QW), lambda i: (i, 0, 0)),
                  pl.BlockSpec((bb, l, 2 * A_KVW), lambda i: (i, 0, 0)),
                  pl.BlockSpec((bb, WINDOW, 2 * A_KVW), lambda i: (i, 0, 0))],
        out_specs=[pl.BlockSpec((bb, l, A_QW), lambda i: (i, 0, 0)),
                   pl.BlockSpec((bb, WINDOW, 2 * A_KVW), lambda i: (i, 0, 0))],
        out_shape=[jax.ShapeDtypeStruct((b, l, A_QW), BF16),
                   jax.ShapeDtypeStruct((b, WINDOW, 2 * A_KVW), F32)],
        compiler_params=_params("arbitrary"),
        name="swa_sample",
    )(sinks, q, kvn, buf)


def _ssd_chunk(conv, z, dt, state, a_row, dsk, gn):
    L = SSD_CHUNK
    gw = D_INNER // B_GROUPS
    hpg = B_HEADS // B_GROUPS
    xs = conv[:, :D_INNER]
    adt = a_row * dt
    r2 = lax.broadcasted_iota(jnp.int32, (L, L), 0)
    c2 = lax.broadcasted_iota(jnp.int32, (L, L), 1)
    lower = r2 >= c2
    acs = jnp.dot(lower.astype(F32), adt, precision=_HI, preferred_element_type=F32)
    acs_t = acs.T
    band = lax.broadcasted_iota(jnp.int32, (L, gw), 1) // B_HEAD_DIM

    ys, states = [], []
    for g in range(B_GROUPS):
        b_g = conv[:, D_INNER + D_STATE * g:D_INNER + D_STATE * (g + 1)].astype(BF16)
        c_g = conv[:, D_INNER + B_GROUPS * D_STATE + D_STATE * g:
                   D_INNER + B_GROUPS * D_STATE + D_STATE * (g + 1)].astype(BF16)
        xs_g = xs[:, gw * g:gw * (g + 1)]

        def expand(mat, g=g):
            out = jnp.broadcast_to(mat[:, hpg * g:hpg * g + 1], (L, gw))
            for r in range(1, hpg):
                out = jnp.where(band == r, jnp.broadcast_to(mat[:, hpg * g + r:hpg * g + r + 1], (L, gw)), out)
            return out

        acs_e = expand(acs)
        xd = xs_g * expand(dt)
        xd_b = xd.astype(BF16)
        gmat = lax.dot_general(c_g, b_g, _NT, preferred_element_type=F32)
        y_diag = None
        for r in range(hpg):
            h = hpg * g + r
            diff = acs[:, h:h + 1] - acs_t[h:h + 1, :]
            lmat = jnp.exp(jnp.where(lower, diff, -jnp.inf))
            yr = jnp.dot((gmat * lmat).astype(BF16), xd_b, preferred_element_type=F32)
            y_diag = yr if y_diag is None else jnp.where(band == r, yr, y_diag)
        s_g = state[gw * g:gw * (g + 1), :]
        y_off = lax.dot_general(c_g, s_g.astype(BF16), _NT, preferred_element_type=F32) * jnp.exp(acs_e)
        decay = jnp.exp(acs_e[L - 1:L, :] - acs_e)
        upd = lax.dot_general((xd * decay).astype(BF16), b_g, _TN, preferred_element_type=F32)
        dec_rows = jnp.concatenate(
            [jnp.broadcast_to(jnp.exp(acs_t[hpg * g + r:hpg * g + r + 1, L - 1:L]), (B_HEAD_DIM, D_STATE))
             for r in range(hpg)], axis=0)
        states.append(s_g * dec_rows + upd)
        ys.append(y_diag + y_off + dsk[:, gw * g:gw * (g + 1)] * xs_g)

    outs = []
    for g in range(B_GROUPS):
        gt = ys[g] * _silu(z[:, gw * g:gw * (g + 1)])
        outs.append(_rms(gt))
    y = jnp.concatenate(outs, axis=-1) * gn
    return y, jnp.concatenate(states, axis=0)


def _conv_from_pad(xp_ref, cw_ref, cb_ref):
    L = SSD_CHUNK
    acc = cb_ref[...] + cw_ref[D_CONV - 1:D_CONV, :] * xp_ref[8:8 + L, :]
    for j in range(D_CONV - 1):
        acc = acc + cw_ref[j:j + 1, :] * xp_ref[5 + j:5 + j + L, :]
    return _silu(acc)


def _ssd_prompt_kernel(xbc_ref, z_ref, dt_ref, cw_ref, cb_ref, dtb_ref, a_ref, dsk_ref, gn_ref,
                       y_ref, st_ref, xp_ref, s_ref):
    i = pl.program_id(0)
    L = SSD_CHUNK

    @pl.when(i == 0)
    def _():
        xp_ref[0:8, :] = jnp.zeros((8, XBC_DIM), F32)
        s_ref[...] = jnp.zeros_like(s_ref)

    xp_ref[8:8 + L, :] = xbc_ref[...]
    conv = _conv_from_pad(xp_ref, cw_ref, cb_ref)
    xp_ref[0:8, :] = xp_ref[L:L + 8, :]
    dt = _softplus(dt_ref[...] + dtb_ref[...])
    y, new_state = _ssd_chunk(conv, z_ref[...], dt, s_ref[...], a_ref[...], dsk_ref[...], gn_ref[...])
    s_ref[...] = new_state
    y_ref[...] = y.astype(BF16)

    @pl.when(i == pl.num_programs(0) - 1)
    def _():
        st_ref[...] = new_state


def _ssd_prompt(xbc, z, dt, cw, cb, dtb, a_row, dsk, gn):
    t = xbc.shape[0]
    L = SSD_CHUNK
    consts = [cw, cb, dtb, a_row, dsk, gn]
    return pl.pallas_call(
        _ssd_prompt_kernel,
        grid=(t // L,),
        in_specs=[pl.BlockSpec((L, XBC_DIM), lambda i: (i, 0)),
                  pl.BlockSpec((L, D_INNER), lambda i: (i, 0)),
                  pl.BlockSpec((L, LANE), lambda i: (i, 0))] + [_const_spec(c) for c in consts],
        out_specs=[pl.BlockSpec((L, D_INNER), lambda i: (i, 0)),
                   pl.BlockSpec((B_HEADS * B_HEAD_DIM, D_STATE), lambda i: (0, 0))],
        out_shape=[jax.ShapeDtypeStruct((t, D_INNER), BF16),
                   jax.ShapeDtypeStruct((B_HEADS * B_HEAD_DIM, D_STATE), F32)],
        scratch_shapes=[pltpu.VMEM((L + 8, XBC_DIM), F32), pltpu.VMEM((B_HEADS * B_HEAD_DIM, D_STATE), F32)],
        compiler_params=_params("arbitrary"),
        name="ssd_prompt",
    )(xbc, z, dt, *consts)


def _ssd_sample_kernel(xbc_ref, z_ref, dt_ref, cs_ref, s0_ref, cw_ref, cb_ref, dtb_ref, a_ref, dsk_ref, gn_ref,
                       y_ref, st_ref, xp_ref, *, l):
    i = pl.program_id(0)
    L = SSD_CHUNK

    @pl.when(i == 0)
    def _():
        xp_ref[...] = jnp.zeros_like(xp_ref)

    xp_ref[5:8, :] = cs_ref[0]
    xp_ref[8:8 + l, :] = xbc_ref[0]
    conv = _conv_from_pad(xp_ref, cw_ref, cb_ref)
    pad = jnp.zeros((L - l, LANE), F32)
    dt = jnp.concatenate([_softplus(dt_ref[0] + dtb_ref[...]), pad], axis=0)
    z = jnp.concatenate([z_ref[0], jnp.zeros((L - l, D_INNER), F32)], axis=0)
    y, new_state = _ssd_chunk(conv, z, dt, s0_ref[0], a_ref[...], dsk_ref[...], gn_ref[...])
    y_ref[0] = y[0:l, :].astype(BF16)
    st_ref[0] = new_state


def _ssd_sample(xbc, z, dt, conv_state, ssm_state, cw, cb, dtb, a_row, dsk, gn):
    b, l, _ = xbc.shape
    L = SSD_CHUNK
    consts = [cw, cb, dtb, a_row, dsk, gn]
    hp = B_HEADS * B_HEAD_DIM
    return pl.pallas_call(
        functools.partial(_ssd_sample_kernel, l=l),
        grid=(b,),
        in_specs=[pl.BlockSpec((1, l, XBC_DIM), lambda i: (i, 0, 0)),
                  pl.BlockSpec((1, l, D_INNER), lambda i: (i, 0, 0)),
                  pl.BlockSpec((1, l, LANE), lambda i: (i, 0, 0)),
                  pl.BlockSpec((1, D_CONV - 1, XBC_DIM), lambda i: (i, 0, 0)),
                  pl.BlockSpec((1, hp, D_STATE), lambda i: (i, 0, 0))] + [_const_spec(c) for c in consts],
        out_specs=[pl.BlockSpec((1, l, D_INNER), lambda i: (i, 0, 0)),
                   pl.BlockSpec((1, hp, D_STATE), lambda i: (i, 0, 0))],
        out_shape=[jax.ShapeDtypeStruct((b, l, D_INNER), BF16),
                   jax.ShapeDtypeStruct((b, hp, D_STATE), F32)],
        scratch_shapes=[pltpu.VMEM((L + 8, XBC_DIM), F32)],
        compiler_params=_params("arbitrary"),
        name="ssd_sample",
    )(xbc, z, dt, conv_state, ssm_state, *consts)


def _proj_res_kernel(*refs, n_in):
    a_refs = refs[:n_in]
    x_ref, g_ref = refs[n_in:n_in + 2]
    w_refs = refs[n_in + 2:2 * n_in + 2]
    o_ref = refs[-1]
    acc = None
    for a_ref, w_ref in zip(a_refs, w_refs):
        part = jnp.dot(a_ref[...], w_ref[...], preferred_element_type=F32)
        acc = part if acc is None else acc + part
    o_ref[...] = x_ref[...] + g_ref[...] * acc


def _proj_res(acts, x, gate, ws):
    t, d = x.shape
    tm = _tile(t, 512)
    n_in = len(acts)
    return pl.pallas_call(
        functools.partial(_proj_res_kernel, n_in=n_in),
        grid=(t // tm,),
        in_specs=[pl.BlockSpec((tm, a.shape[1]), lambda i: (i, 0)) for a in acts]
        + [pl.BlockSpec((tm, d), lambda i: (i, 0)), _row_spec(gate, tm)] + [_const_spec(w) for w in ws],
        out_specs=pl.BlockSpec((tm, d), lambda i: (i, 0)),
        out_shape=jax.ShapeDtypeStruct((t, d), F32),
        compiler_params=_params("arbitrary"),
        name="proj_residual",
    )(*acts, x, gate, *ws)


def _swiglu_kernel(x_ref, sh_ref, sc_ref, g_ref, wgu_ref, wd_ref, o_ref, *, n_chunk):
    x = x_ref[...]
    h = (_rms(x) * (1.0 + sc_ref[...]) + sh_ref[...]).astype(BF16)
    f = wd_ref.shape[0]
    tf = f // n_chunk
    acc = None
    for c in range(n_chunk):
        gp = jnp.dot(h, wgu_ref[:, c * tf:(c + 1) * tf], preferred_element_type=F32)
        up = jnp.dot(h, wgu_ref[:, f + c * tf:f + (c + 1) * tf], preferred_element_type=F32)
        a = (_silu(gp) * up).astype(BF16)
        part = jnp.dot(a, wd_ref[c * tf:(c + 1) * tf, :], preferred_element_type=F32)
        acc = part if acc is None else acc + part
    o_ref[...] = x + g_ref[...] * acc


def _swiglu(x, sh, sc, gate, wgu, wd):
    t, d = x.shape
    tm = _tile(t, 512)
    return pl.pallas_call(
        functools.partial(_swiglu_kernel, n_chunk=2),
        grid=(t // tm,),
        in_specs=[pl.BlockSpec((tm, d), lambda i: (i, 0)), _row_spec(sh, tm), _row_spec(sc, tm), _row_spec(gate, tm),
                  _const_spec(wgu), _const_spec(wd)],
        out_specs=pl.BlockSpec((tm, d), lambda i: (i, 0)),
        out_shape=jax.ShapeDtypeStruct((t, d), F32),
        compiler_params=_params("arbitrary"),
        name="swiglu",
    )(x, sh, sc, gate, wgu, wd)


def _rope_lanes(blk, t_ref):
    half = C_ROPE // 2
    return (blk * t_ref[0] + pltpu.roll(blk, half, 1) * t_ref[1]
            + pltpu.roll(blk, LANE - half, 1) * t_ref[2])


def _od_latents(x_ref, sh_ref, sc_ref, w_in_ref, qn_ref, kvn_ref):
    x = x_ref[...]
    h = (_rms(x) * (1.0 + sc_ref[...]) + sh_ref[...]).astype(BF16)
    y = jnp.dot(h, w_in_ref[...], preferred_element_type=F32)
    cqn = (_rms(y[:, :Q_LORA]) * qn_ref[...]).astype(BF16)
    ckvn = _rms(y[:, Q_LORA:Q_LORA + KV_LORA]) * kvn_ref[...]
    return cqn, ckvn, y[:, Q_LORA + KV_LORA:]


def _od_in_prompt_kernel(x_ref, sh_ref, sc_ref, w_in_ref, qn_ref, kvn_ref, wuq_ref, wuk_ref, wuv_ref,
                         tq_ref, tk_ref, q_out, k_out, v_out, ckv_out, kpe_out):
    cqn, ckvn, kpe_pad = _od_latents(x_ref, sh_ref, sc_ref, w_in_ref, qn_ref, kvn_ref)
    ckv_out[...] = ckvn
    ckb = ckvn.astype(BF16)
    qf = jnp.dot(cqn, wuq_ref[...], preferred_element_type=F32)
    kf = jnp.dot(ckb, wuk_ref[...], preferred_element_type=F32)
    v_out[0] = lax.dot_general(wuv_ref[...], ckb, _NT, preferred_element_type=F32).astype(BF16)
    kpr = _rope_lanes(kpe_pad, tk_ref)
    kpe_out[...] = kpr[:, C_NOPE:C_NOPE + C_ROPE]
    for hh in range(C_HEADS):
        q_out[hh] = _rope_lanes(qf[:, LANE * hh:LANE * (hh + 1)], tq_ref).astype(BF16)
        k_out[hh] = (kf[:, LANE * hh:LANE * (hh + 1)] + kpr).astype(BF16)


def _od_in_prompt(x, sh, sc, w_in, qn, kvn, wuq, wuk, wuv, tq, tk):
    t, d = x.shape
    tm = _tile(t, MLA_TK)
    consts = [w_in, qn, kvn, wuq, wuk, wuv]
    return pl.pallas_call(
        _od_in_prompt_kernel,
        grid=(t // tm,),
        in_specs=[pl.BlockSpec((tm, d), lambda i: (i, 0)), _row_spec(sh, tm), _row_spec(sc, tm)]
        + [_const_spec(c) for c in consts]
        + [pl.BlockSpec((3, tm, LANE), lambda i: (0, i, 0)), pl.BlockSpec((3, tm, LANE), lambda i: (0, i, 0))],
        out_specs=[pl.BlockSpec((C_HEADS, tm, LANE), lambda i: (0, i, 0)),
                   pl.BlockSpec((C_HEADS, tm, LANE), lambda i: (0, i, 0)),
                   pl.BlockSpec((1, C_HEADS * C_V, tm), lambda i: (i, 0, 0)),
                   pl.BlockSpec((tm, KV_LORA), lambda i: (i, 0)),
                   pl.BlockSpec((tm, C_ROPE), lambda i: (i, 0))],
        out_shape=[jax.ShapeDtypeStruct((C_HEADS, t, LANE), BF16),
                   jax.ShapeDtypeStruct((C_HEADS, t, LANE), BF16),
                   jax.ShapeDtypeStruct((t // tm, C_HEADS * C_V, tm), BF16),
                   jax.ShapeDtypeStruct((t, KV_LORA), F32),
                   jax.ShapeDtypeStruct((t, C_ROPE), F32)],
        compiler_params=_params("arbitrary"),
        name="mla_proj_prompt",
    )(x, sh, sc, *consts, tq, tk)


def _od_in_sample_kernel(x_ref, sh_ref, sc_ref, w_in_ref, qn_ref, kvn_ref, wuq_ref, wukt_ref,
                         tq_ref, tk_ref, ql_out, qp_out, ckv_out, kpe_out, kpp_out):
    cqn, ckvn, kpe_pad = _od_latents(x_ref, sh_ref, sc_ref, w_in_ref, qn_ref, kvn_ref)
    ckv_out[...] = ckvn
    kpr = _rope_lanes(kpe_pad, tk_ref)
    kpe_out[...] = kpr[:, 0:C_ROPE]
    kpp_out[...] = kpr
    qf = jnp.dot(cqn, wuq_ref[...], preferred_element_type=F32)
    nq = C_HEADS * LANE
    for hh in range(C_HEADS):
        qn_h = (qf[:, LANE * hh:LANE * (hh + 1)] * MLA_SCALE).astype(BF16)
        ql = jnp.dot(qn_h, wukt_ref[hh], preferred_element_type=F32)
        ql_out[:, KV_LORA * hh:KV_LORA * (hh + 1)] = ql.astype(BF16)
        qp_out[:, LANE * hh:LANE * (hh + 1)] = _rope_lanes(qf[:, nq + LANE * hh:nq + LANE * (hh + 1)], tq_ref).astype(BF16)


def _od_in_sample(x, sh, sc, w_in, qn, kvn, wuq, wukt, tq, tk):
    t, d = x.shape
    tm = _tile(t, 512)
    consts = [w_in, qn, kvn, wuq, wukt]
    widths = [(C_HEADS * KV_LORA, BF16), (C_HEADS * LANE, BF16), (KV_LORA, F32), (C_ROPE, F32), (LANE, F32)]
    return pl.pallas_call(
        _od_in_sample_kernel,
        grid=(t // tm,),
        in_specs=[pl.BlockSpec((tm, d), lambda i: (i, 0)), _row_spec(sh, tm), _row_spec(sc, tm)]
        + [_const_spec(c) for c in consts]
        + [pl.BlockSpec((3, tm, LANE), lambda i: (0, i, 0)), pl.BlockSpec((3, tm, LANE), lambda i: (0, i, 0))],
        out_specs=[pl.BlockSpec((tm, n), lambda i: (i, 0)) for n, _ in widths],
        out_shape=[jax.ShapeDtypeStruct((t, n), dt) for n, dt in widths],
        compiler_params=_params("arbitrary"),
        name="mla_proj_sample",
    )(x, sh, sc, *consts, tq, tk)


def _flash_kernel(q_ref, k_ref, vt_ref, o_ref, m_ref, l_ref, acc_ref, *, tq, tk):
    qi = pl.program_id(1)
    m_ref[...] = jnp.full(m_ref.shape, -jnp.inf, F32)
    l_ref[...] = jnp.zeros(l_ref.shape, F32)
    acc_ref[...] = jnp.zeros(acc_ref.shape, F32)

    def scores(j, a, masked):
        start = pl.multiple_of(j * tk, tk)
        st = lax.dot_general(k_ref[a, pl.ds(start, tk), :], q_ref[a], _NT, preferred_element_type=F32)
        if masked:
            krow = lax.broadcasted_iota(jnp.int32, (tk, tq), 0) + (j * tk - qi * tq)
            qcol = lax.broadcasted_iota(jnp.int32, (tk, tq), 1)
            st = jnp.where(krow <= qcol, st, -jnp.inf)
        return st

    def chunk(j, masked):
        for a in range(2):
            st = scores(j, a, masked)
            m_prev = m_ref[a]
            m_new = jnp.maximum(m_prev, jnp.max(st, axis=0, keepdims=True))
            alpha = jnp.exp2(m_prev - m_new)
            p = jnp.exp2(st - m_new)
            l_ref[a] = alpha * l_ref[a] + jnp.sum(p, axis=0, keepdims=True)
            pv = jnp.dot(vt_ref[j, C_V * a:C_V * (a + 1), :], p.astype(BF16), preferred_element_type=F32)
            acc_ref[a] = alpha * acc_ref[a] + pv
            m_ref[a] = m_new

    n_full = (qi * tq) // tk

    def body(j2, carry):
        chunk(2 * j2, False)
        chunk(2 * j2 + 1, False)
        return carry

    lax.fori_loop(0, n_full // 2, body, 0)

    @pl.when(n_full % 2 == 1)
    def _():
        chunk(n_full - 1, False)

    for dd in range(max(tq // tk, 1)):
        chunk(n_full + dd, True)
    o_t = jnp.concatenate([acc_ref[0] / l_ref[0], acc_ref[1] / l_ref[1]], axis=0)
    o_ref[...] = o_t.T.astype(BF16)


def _flash_prompt(qh, kh, vt):
    nk, _, tk = vt.shape
    t = nk * tk
    tq = _tile(t, MLA_TQ)
    return pl.pallas_call(
        functools.partial(_flash_kernel, tq=tq, tk=tk),
        grid=(C_HEADS // 2, t // tq),
        in_specs=[pl.BlockSpec((2, tq, LANE), lambda hp, qi: (hp, qi, 0)),
                  pl.BlockSpec((2, t, LANE), lambda hp, qi: (hp, 0, 0)),
                  pl.BlockSpec((nk, 2 * C_V, tk), lambda hp, qi: (0, hp, 0))],
        out_specs=pl.BlockSpec((tq, LANE), lambda hp, qi: (qi, hp)),
        out_shape=jax.ShapeDtypeStruct((t, C_HEADS * C_V), BF16),
        scratch_shapes=[pltpu.VMEM((2, 1, tq), F32), pltpu.VMEM((2, 1, tq), F32), pltpu.VMEM((2, C_V, tq), F32)],
        compiler_params=_params("arbitrary", "arbitrary"),
        name="mla_flash_prompt",
    )(qh, kh, vt)


def _paged_kernel(pt_ref, ql_ref, qp_ref, cn_ref, kn_ref, ckv_hbm, kpt_hbm, o_ref,
                  ck_buf, kp_buf, ck_sem, kp_sem, m_ref, l_ref, acc_ref, *, ch, l):
    b = pl.program_id(0)
    nb = pl.num_programs(0)
    n_chunks = pt_ref.shape[1] // ch
    nq = ql_ref.shape[1]

    def ck_copy(bi, c, p, slot):
        page = pt_ref[bi, c * ch + p]
        return pltpu.make_async_copy(ckv_hbm.at[0, page], ck_buf.at[slot, pl.ds(p * PAGE_SIZE, PAGE_SIZE), :],
                                     ck_sem.at[slot])

    def kp_copy(bi, c, p, slot):
        page = pt_ref[bi, c * ch + p]
        return pltpu.make_async_copy(kpt_hbm.at[0, page], kp_buf.at[slot, 0:C_ROPE, pl.ds(p * PAGE_SIZE, PAGE_SIZE)],
                                     kp_sem.at[slot])

    def start_chunk(bi, c, slot):
        for p in range(ch):
            ck_copy(bi, c, p, slot).start()
            kp_copy(bi, c, p, slot).start()

    def wait_chunk(slot):
        for p in range(ch):
            ck_copy(0, 0, p, slot).wait()
            kp_copy(0, 0, p, slot).wait()

    @pl.when(b == 0)
    def _():
        kp_buf[...] = jnp.zeros_like(kp_buf)
        start_chunk(0, 0, 0)

    m_ref[...] = jnp.full(m_ref.shape, -jnp.inf, F32)
    l_ref[...] = jnp.zeros(l_ref.shape, F32)
    acc_ref[...] = jnp.zeros(acc_ref.shape, F32)
    ql = ql_ref[0]
    qp = qp_ref[0]

    def update(k, s, values):
        m_prev = m_ref[k]
        m_new = jnp.maximum(m_prev, jnp.max(s, axis=-1, keepdims=True))
        alpha = jnp.exp(m_prev - m_new)
        p = jnp.exp(s - m_new)
        l_ref[k] = alpha * l_ref[k] + jnp.sum(p, axis=-1, keepdims=True)
        acc_ref[k] = alpha * acc_ref[k] + jnp.dot(p.astype(BF16), values, preferred_element_type=F32)
        m_ref[k] = m_new

    n_split = m_ref.shape[0]
    span = ch * PAGE_SIZE // n_split

    def step(c, slot):
        @pl.when(c + 1 < n_chunks)
        def _():
            start_chunk(b, c + 1, 1 - slot)

        @pl.when((c + 1 == n_chunks) & (b + 1 < nb))
        def _():
            start_chunk(b + 1, 0, 1 - slot)

        wait_chunk(slot)
        for k in range(n_split):
            ck = ck_buf[slot, k * span:(k + 1) * span, :].astype(BF16)
            kp = kp_buf[slot, :, k * span:(k + 1) * span].astype(BF16)
            s = (lax.dot_general(ql, ck, _NT, preferred_element_type=F32)
                 + jnp.dot(qp, kp, preferred_element_type=F32))
            update(k, s, ck)

    def pair(c2, carry):
        step(2 * c2, 0)
        step(2 * c2 + 1, 1)
        return carry

    lax.fori_loop(0, n_chunks // 2, pair, 0)

    cn = jnp.concatenate([cn_ref[0], jnp.zeros((LANE - l, KV_LORA), F32)], axis=0).astype(BF16)
    kn_t = jnp.concatenate([kn_ref[0], jnp.zeros((LANE - l, LANE), F32)], axis=0).T.astype(BF16)
    s = lax.dot_general(ql, cn, _NT, preferred_element_type=F32) + jnp.dot(qp, kn_t, preferred_element_type=F32)
    qtok = lax.broadcasted_iota(jnp.int32, (nq, LANE), 0) // C_HEADS
    kcol = lax.broadcasted_iota(jnp.int32, (nq, LANE), 1)
    update(0, jnp.where(kcol <= qtok, s, -jnp.inf), cn)
    m_all = m_ref[0]
    for k in range(1, n_split):
        m_all = jnp.maximum(m_all, m_ref[k])
    num = jnp.zeros((nq, KV_LORA), F32)
    den = jnp.zeros((nq, 1), F32)
    for k in range(n_split):
        w = jnp.exp(m_ref[k] - m_all)
        num = num + w * acc_ref[k]
        den = den + w * l_ref[k]
    o_ref[0] = (num / den).astype(BF16)


def _paged_attention(page_table, ql, qp, ckv_new, kpe_new_pad, cache_ckv, cache_kpe_t):
    b, nq, _ = ql.shape
    l = ckv_new.shape[1]
    n_pages = page_table.shape[1]
    ch = max(c for c in (16, 8, 4, 2, 1) if n_pages % (2 * c) == 0)
    grid_spec = pltpu.PrefetchScalarGridSpec(
        num_scalar_prefetch=1,
        grid=(b,),
        in_specs=[pl.BlockSpec((1, nq, KV_LORA), lambda bi, pt: (bi, 0, 0)),
                  pl.BlockSpec((1, nq, LANE), lambda bi, pt: (bi, 0, 0)),
                  pl.BlockSpec((1, l, KV_LORA), lambda bi, pt: (bi, 0, 0)),
                  pl.BlockSpec((1, l, LANE), lambda bi, pt: (bi, 0, 0)),
                  pl.BlockSpec(memory_space=pl.ANY),
                  pl.BlockSpec(memory_space=pl.ANY)],
        out_specs=pl.BlockSpec((1, nq, KV_LORA), lambda bi, pt: (bi, 0, 0)),
        scratch_shapes=[pltpu.VMEM((2, ch * PAGE_SIZE, KV_LORA), F32),
                        pltpu.VMEM((2, LANE, ch * PAGE_SIZE), F32),
                        pltpu.SemaphoreType.DMA((2,)),
                        pltpu.SemaphoreType.DMA((2,)),
                        pltpu.VMEM((PAGED_STREAMS, nq, 1), F32), pltpu.VMEM((PAGED_STREAMS, nq, 1), F32),
                        pltpu.VMEM((PAGED_STREAMS, nq, KV_LORA), F32)],
    )
    return pl.pallas_call(
        functools.partial(_paged_kernel, ch=ch, l=l),
        grid_spec=grid_spec,
        out_shape=jax.ShapeDtypeStruct((b, nq, KV_LORA), BF16),
        compiler_params=_params("arbitrary"),
        name="mla_paged_sample",
    )(page_table, ql, qp, ckv_new, kpe_new_pad, cache_ckv, cache_kpe_t)


def _mla_out_sample_kernel(o_ref, x_ref, g_ref, wuv_ref, wo_ref, out_ref):
    parts = []
    for pr in range(C_HEADS // 2):
        parts.append(jnp.dot(o_ref[:, 2 * KV_LORA * pr:2 * KV_LORA * (pr + 1)], wuv_ref[pr],
                             preferred_element_type=F32))
    attn = jnp.concatenate(parts, axis=-1).astype(BF16)
    out_ref[...] = x_ref[...] + g_ref[...] * jnp.dot(attn, wo_ref[...], preferred_element_type=F32)


def _mla_out_sample(o_lat, x, gate, wuv_pairs, wo):
    t, d = x.shape
    tm = _tile(t, 512)
    return pl.pallas_call(
        _mla_out_sample_kernel,
        grid=(t // tm,),
        in_specs=[pl.BlockSpec((tm, o_lat.shape[1]), lambda i: (i, 0)), pl.BlockSpec((tm, d), lambda i: (i, 0)),
                  _row_spec(gate, tm), _const_spec(wuv_pairs), _const_spec(wo)],
        out_specs=pl.BlockSpec((tm, d), lambda i: (i, 0)),
        out_shape=jax.ShapeDtypeStruct((t, d), F32),
        compiler_params=_params("arbitrary"),
        name="mla_out_sample",
    )(o_lat, x, gate, wuv_pairs, wo)


def _route_kernel(x_ref, sh_ref, sc_ref, r_ref, h_ref, gate_ref):
    x = x_ref[...]
    h = _rms(x) * (1.0 + sc_ref[...]) + sh_ref[...]
    h_ref[...] = h.astype(BF16)
    logits = jnp.dot(h, r_ref[...], precision=_HI, preferred_element_type=F32)
    lane = lax.broadcasted_iota(jnp.int32, logits.shape, 1)
    logits = jnp.where(lane < N_EXPERTS, logits, -jnp.inf)
    v1 = jnp.max(logits, axis=-1, keepdims=True)
    i1 = jnp.min(jnp.where(logits == v1, lane, LANE), axis=-1, keepdims=True)
    rest = jnp.where(lane == i1, -jnp.inf, logits)
    v2 = jnp.max(rest, axis=-1, keepdims=True)
    i2 = jnp.min(jnp.where(rest == v2, lane, LANE), axis=-1, keepdims=True)
    e2 = jnp.exp(v2 - v1)
    g1 = 1.0 / (1.0 + e2)
    g2 = e2 / (1.0 + e2)
    gate_ref[...] = jnp.where(lane == i1, g1, 0.0) + jnp.where(lane == i2, g2, 0.0)


def _route(x, sh, sc, router_pad):
    t, d = x.shape
    tm = _tile(t, 512)
    return pl.pallas_call(
        _route_kernel,
        grid=(t // tm,),
        in_specs=[pl.BlockSpec((tm, d), lambda i: (i, 0)), _row_spec(sh, tm), _row_spec(sc, tm), _const_spec(router_pad)],
        out_specs=[pl.BlockSpec((tm, d), lambda i: (i, 0)), pl.BlockSpec((tm, LANE), lambda i: (i, 0))],
        out_shape=[jax.ShapeDtypeStruct((t, d), BF16), jax.ShapeDtypeStruct((t, LANE), F32)],
        compiler_params=_params("arbitrary"),
        name="moe_route",
    )(x, sh, sc, router_pad)


def _moe_dense_kernel(h_ref, gate_ref, x_ref, g2_ref, fn_ref, wg_ref, wu_ref, wd_ref, o_ref, acc_ref):
    e = pl.program_id(1)
    c = pl.program_id(2)

    @pl.when((e == 0) & (c == 0))
    def _():
        acc_ref[...] = jnp.zeros_like(acc_ref)

    h = h_ref[...]
    gate = gate_ref[...]
    lane = lax.broadcasted_iota(jnp.int32, gate.shape, 1)
    gcol = jnp.sum(jnp.where(lane == e, gate, 0.0), axis=-1, keepdims=True)
    gp = jnp.dot(h, wg_ref[0], preferred_element_type=F32)
    up = jnp.dot(h, wu_ref[0], preferred_element_type=F32)
    a = (_silu(gp) * up * gcol).astype(BF16)
    acc_ref[...] += jnp.dot(a, wd_ref[0], preferred_element_type=F32)

    @pl.when((e == pl.num_programs(1) - 1) & (c == pl.num_programs(2) - 1))
    def _():
        xo = x_ref[...] + g2_ref[...] * acc_ref[...]
        o_ref[...] = _rms(xo) * fn_ref[...]


def _moe_dense(h, gate, x, g2, fn, wgu, wd):
    t, d = x.shape
    ne, f, _ = wd.shape
    tm = _tile(t, 512)
    tf = 512
    nc = f // tf
    return pl.pallas_call(
        _moe_dense_kernel,
        grid=(t // tm, ne, nc),
        in_specs=[pl.BlockSpec((tm, d), lambda i, e, c: (i, 0)),
                  pl.BlockSpec((tm, LANE), lambda i, e, c: (i, 0)),
                  pl.BlockSpec((tm, d), lambda i, e, c: (i, 0)),
                  (pl.BlockSpec((1, d), lambda i, e, c: (0, 0)) if g2.shape[0] == 1
                   else pl.BlockSpec((tm, d), lambda i, e, c: (i, 0))),
                  pl.BlockSpec((1, d), lambda i, e, c: (0, 0)),
                  pl.BlockSpec((1, d, tf), lambda i, e, c: (e, 0, c)),
                  pl.BlockSpec((1, d, tf), lambda i, e, c: (e, 0, nc + c)),
                  pl.BlockSpec((1, tf, d), lambda i, e, c: (e, c, 0))],
        out_specs=pl.BlockSpec((tm, d), lambda i, e, c: (i, 0)),
        out_shape=jax.ShapeDtypeStruct((t, d), F32),
        scratch_shapes=[pltpu.VMEM((tm, d), F32)],
        compiler_params=_params("arbitrary", "arbitrary", "arbitrary"),
        name="moe_dense",
    )(h, gate, x, g2, fn, wgu, wgu, wd)


def _rope_tables(pos, lo, reps, scale):
    half = C_ROPE // 2
    freqs = ROPE_THETA ** (-2.0 * jnp.arange(half, dtype=F32) / C_ROPE)
    ang = pos.astype(F32)[:, None] * freqs[None]
    cos, sin = jnp.cos(ang), jnp.sin(ang)
    n = pos.shape[0]
    zeros = jnp.zeros((n, half), F32)
    c_grp = jnp.concatenate([cos, cos], axis=1)
    s1_grp = jnp.concatenate([zeros, sin], axis=1)
    s2_grp = jnp.concatenate([-sin, zeros], axis=1)

    def lay(grp, fill):
        body = jnp.tile(grp, (1, reps))
        left = jnp.full((n, lo), fill, F32)
        right = jnp.zeros((n, LANE - lo - reps * C_ROPE), F32)
        return jnp.concatenate([left, body, right], axis=1)

    return jnp.stack([lay(c_grp, 1.0), lay(s1_grp, 0.0), lay(s2_grp, 0.0)]) * scale


def _pad_cols(w, n):
    return jnp.pad(w, ((0, 0), (0, n - w.shape[1])))


def kernel(x_prompt, x_sample, c_prompt, c_sample, state_swa_kv, state_conv, state_ssm, cache_ckv, cache_kpe, page_table,
           ev_mod_w, ev_mod_b, ev_w_in, ev_sinks, ev_conv_w, ev_conv_b, ev_dt_bias, ev_a_log, ev_d_skip, ev_gnorm,
           ev_w_out, ev_w_gu, ev_w_down, od_mod_w, od_mod_b, od_w_in, od_qnorm, od_kvnorm, od_w_uq, od_w_uk, od_w_uv,
           od_w_out, od_router, od_w_gu, od_w_down, final_norm):
    d = D_MODEL
    _, tp, _ = x_prompt.shape
    bs, ls, _ = x_sample.shape
    ts = bs * ls
    n_pages = page_table.shape[1]
    past = n_pages * PAGE_SIZE
    assert state_swa_kv.shape[2] == WINDOW and ls <= 8 and ls >= D_CONV - 1

    xp = x_prompt.reshape(tp, d)
    xs = x_sample.reshape(ts, d)

    n_c = 1 + bs
    n_cp = -(-n_c // 8) * 8
    c_all = jnp.pad(jnp.concatenate([c_prompt, c_sample], axis=0), ((0, n_cp - n_c), (0, 0)))

    def mods(w, b):
        m = _ada_mod(c_all, w, b)
        mp = [m[0:1, k * d:(k + 1) * d] for k in range(MOD_SLOTS)]
        ms = [jnp.repeat(m[1:n_c, k * d:(k + 1) * d], ls, axis=0) for k in range(MOD_SLOTS)]
        return mp, ms

    i = 0
    mp, ms = mods(ev_mod_w[i], ev_mod_b[i])
    w_in = ev_w_in[i]
    o_k, o_v, o_z, o_x, o_dt = A_QW, A_QW + A_KVW, A_QW + 2 * A_KVW, A_QW + 2 * A_KVW + D_INNER, A_QW + 2 * A_KVW + D_INNER + XBC_DIM
    w_in_p = jnp.concatenate([w_in[:, :o_k], w_in[:, o_z:o_x], w_in[:, o_x:o_dt], w_in[:, o_k:o_z],
                              _pad_cols(w_in[:, o_dt:], LANE)], axis=1).astype(BF16)
    sinks = ev_sinks[i].astype(F32)
    cw = ev_conv_w[i]
    cb = ev_conv_b[i].reshape(1, XBC_DIM)
    dtb = _pad_cols(ev_dt_bias[i].reshape(1, B_HEADS).astype(F32), LANE)
    a_row = _pad_cols(-jnp.exp(ev_a_log[i].astype(F32)).reshape(1, B_HEADS), LANE)
    dsk = jnp.repeat(ev_d_skip[i].astype(F32), B_HEAD_DIM).reshape(1, D_INNER)
    gn = ev_gnorm[i].reshape(1, D_INNER)
    w_out = ev_w_out[i].astype(BF16)
    w_gu = ev_w_gu[i].astype(BF16)
    w_dn = ev_w_down[i].astype(BF16)
    ssd_consts = (cw, cb, dtb, a_row, dsk, gn)

    q, z, xbc, kv, dt = _ev_in(xp, mp[0], mp[1], w_in_p)
    attn = _swa_prompt(q, kv, sinks)
    ssm, ssm_state_p = _ssd_prompt(xbc, z, dt, *ssd_consts)
    xp = _proj_res([attn, ssm], xp, mp[2], [w_out[:A_QW], w_out[A_QW:]])
    xp = _swiglu(xp, mp[3], mp[4], mp[5], w_gu, w_dn)
    swa_kv_prompt = kv[tp - WINDOW:].reshape(1, 1, WINDOW, 2, A_KV_HEADS, A_HEAD_DIM)
    conv_prompt = xbc[tp - (D_CONV - 1):].reshape(1, 1, D_CONV - 1, XBC_DIM)
    ssm_prompt = ssm_state_p.reshape(1, 1, B_HEADS, B_HEAD_DIM, D_STATE)

    q, z, xbc, kv, dt = _ev_in(xs, ms[0], ms[1], w_in_p)
    buf = state_swa_kv[i].reshape(bs, WINDOW, 2 * A_KVW)
    attn, nbuf = _swa_sample(q.reshape(bs, ls, A_QW), kv.reshape(bs, ls, 2 * A_KVW), buf, sinks)
    xbc3 = xbc.reshape(bs, ls, XBC_DIM)
    ssm, ssm_state_s = _ssd_sample(xbc3, z.reshape(bs, ls, D_INNER), dt.reshape(bs, ls, LANE), state_conv[i],
                                   state_ssm[i].reshape(bs, B_HEADS * B_HEAD_DIM, D_STATE), *ssd_consts)
    xs = _proj_res([attn.reshape(ts, A_QW), ssm.reshape(ts, D_INNER)], xs, ms[2], [w_out[:A_QW], w_out[A_QW:]])
    xs = _swiglu(xs, ms[3], ms[4], ms[5], w_gu, w_dn)
    swa_kv_sample = nbuf.reshape(1, bs, WINDOW, 2, A_KV_HEADS, A_HEAD_DIM)
    conv_sample = xbc3[:, ls - (D_CONV - 1):].reshape(1, bs, D_CONV - 1, XBC_DIM)
    ssm_sample = ssm_state_s.reshape(1, bs, B_HEADS, B_HEAD_DIM, D_STATE)

    mp, ms = mods(od_mod_w[i], od_mod_b[i])
    w_in = od_w_in[i]
    w_cq, w_ckv, w_kpe = w_in[:, :Q_LORA], w_in[:, Q_LORA:Q_LORA + KV_LORA], w_in[:, Q_LORA + KV_LORA:]
    qn = od_qnorm[i].reshape(1, Q_LORA)
    kvn = od_kvnorm[i].reshape(1, KV_LORA)
    w_uq = od_w_uq[i].reshape(Q_LORA, C_HEADS, C_NOPE + C_ROPE)
    w_uk = od_w_uk[i]
    w_uv = od_w_uv[i]
    w_o = od_w_out[i].astype(BF16)

    w_in_pp = jnp.concatenate([w_cq, w_ckv, jnp.pad(w_kpe, ((0, 0), (C_NOPE, LANE - C_NOPE - C_ROPE)))], axis=1).astype(BF16)
    wuq_p = jnp.pad(w_uq, ((0, 0), (0, 0), (0, LANE - C_NOPE - C_ROPE))).reshape(Q_LORA, C_HEADS * LANE).astype(BF16)
    wuk_p = jnp.pad(w_uk, ((0, 0), (0, 0), (0, LANE - C_NOPE))).reshape(KV_LORA, C_HEADS * LANE).astype(BF16)
    wuv_p = w_uv.reshape(KV_LORA, C_HEADS * C_V).T.astype(BF16)
    pos_p = jnp.arange(tp, dtype=jnp.int32)
    tq_p = _rope_tables(pos_p, C_NOPE, 1, MLA_SCALE * LOG2E)
    tk_p = _rope_tables(pos_p, C_NOPE, 1, 1.0)
    qh, kh, v, ckv_p, kpe_p = _od_in_prompt(xp, mp[0], mp[1], w_in_pp, qn, kvn, wuq_p, wuk_p, wuv_p, tq_p, tk_p)
    attn = _flash_prompt(qh, kh, v)
    xp = _proj_res([attn], xp, mp[2], [w_o])

    w_in_ps = jnp.concatenate([w_cq, w_ckv, _pad_cols(w_kpe, LANE)], axis=1).astype(BF16)
    wq_nope = jnp.pad(w_uq[:, :, :C_NOPE], ((0, 0), (0, 0), (0, LANE - C_NOPE))).reshape(Q_LORA, C_HEADS * LANE)
    wq_rope = jnp.pad(w_uq[:, :, C_NOPE:], ((0, 0), (0, 0), (0, LANE - C_ROPE))).reshape(Q_LORA, C_HEADS * LANE)
    wuq_s = jnp.concatenate([wq_nope, wq_rope], axis=1).astype(BF16)
    wukt_s = jnp.pad(jnp.transpose(w_uk, (1, 2, 0)), ((0, 0), (0, LANE - C_NOPE), (0, 0))).astype(BF16)
    pos_s = jnp.tile(past + jnp.arange(ls, dtype=jnp.int32), bs)
    tq_s = _rope_tables(pos_s, 0, 1, MLA_SCALE)
    tk_s = _rope_tables(pos_s, 0, 1, 1.0)
    ql, qp, ckv_s, kpe_s, kpe_pad_s = _od_in_sample(xs, ms[0], ms[1], w_in_ps, qn, kvn, wuq_s, wukt_s, tq_s, tk_s)
    nq = ls * C_HEADS
    o_lat = _paged_attention(page_table, ql.reshape(bs, nq, KV_LORA), qp.reshape(bs, nq, LANE),
                             ckv_s.reshape(bs, ls, KV_LORA), kpe_pad_s.reshape(bs, ls, LANE),
                             cache_ckv[i:i + 1], jnp.swapaxes(cache_kpe[i:i + 1], 2, 3))
    wuv_h = jnp.transpose(w_uv, (1, 0, 2)).reshape(C_HEADS // 2, 2, KV_LORA, C_V)
    zero = jnp.zeros((C_HEADS // 2, KV_LORA, C_V), F32)
    wuv_pairs = jnp.concatenate([jnp.concatenate([wuv_h[:, 0], zero], axis=2),
                                 jnp.concatenate([zero, wuv_h[:, 1]], axis=2)], axis=1).astype(BF16)
    xs = _mla_out_sample(o_lat.reshape(ts, C_HEADS * KV_LORA), xs, ms[2], wuv_pairs, w_o)

    router_pad = _pad_cols(od_router[i].astype(F32), LANE)
    wgu_e = od_w_gu[i].astype(BF16)
    wdn_e = od_w_down[i].astype(BF16)
    fn = final_norm.reshape(1, d).astype(F32)
    h, gate = _route(xp, mp[3], mp[4], router_pad)
    y_prompt = _moe_dense(h, gate, xp, mp[5], fn, wgu_e, wdn_e)
    h, gate = _route(xs, ms[3], ms[4], router_pad)
    y_sample = _moe_dense(h, gate, xs, ms[5], fn, wgu_e, wdn_e)

    return (y_prompt.reshape(1, tp, d), y_sample.reshape(bs, ls, d),
            swa_kv_prompt, swa_kv_sample, conv_prompt, conv_sample, ssm_prompt, ssm_sample,
            ckv_p.reshape(1, 1, tp, KV_LORA), ckv_s.reshape(1, bs, ls, KV_LORA),
            kpe_p.reshape(1, 1, tp, C_ROPE), kpe_s.reshape(1, bs, ls, C_ROPE))
```

```python
import functools
import math

import jax
import jax.numpy as jnp
from jax import lax
from jax.experimental import pallas as pl
from jax.experimental.pallas import tpu as pltpu

F32 = jnp.float32
BF16 = jnp.bfloat16

D_MODEL = 1024
EPS = 1e-6
MOD_SLOTS = 6

A_HEAD_DIM = 64
A_HEADS = 8
A_KV_HEADS = 2
A_REP = A_HEADS // A_KV_HEADS
WINDOW = 128
A_QW = A_HEADS * A_HEAD_DIM
A_KVW = A_KV_HEADS * A_HEAD_DIM

D_INNER = 512
B_HEAD_DIM = 64
B_HEADS = 8
B_GROUPS = 2
D_STATE = 128
D_CONV = 4
SSD_CHUNK = 128
XBC_DIM = D_INNER + 2 * B_GROUPS * D_STATE

C_HEADS = 16
C_NOPE = 64
C_ROPE = 32
C_V = 64
Q_LORA = 384
KV_LORA = 256
ROPE_THETA = 10000.0
MLA_SCALE = (C_NOPE + C_ROPE) ** -0.5
PAGE_SIZE = 128

D_FF = 2816
N_EXPERTS = 8
D_FF_EXPERT = 3584

LANE = 128
MLA_TK = 512
MLA_TQ = 512
LOG2E = 1.4426950408889634
MOE_TM = 512
PAGED_STREAMS = 1
VMEM_LIMIT = 56 * 1024 * 1024

_HI = lax.Precision.HIGHEST
_NT = (((1,), (1,)), ((), ()))
_TN = (((0,), (0,)), ((), ()))


def _params(*sem):
    return pltpu.CompilerParams(dimension_semantics=sem, vmem_limit_bytes=VMEM_LIMIT)


def _tile(n, pref):
    t = min(n, pref)
    while n % t:
        t -= 8
    return t


def _rms(x):
    return x * lax.rsqrt(jnp.mean(x * x, axis=-1, keepdims=True) + EPS)


def _silu(x):
    return x * jax.nn.sigmoid(x)


def _softplus(x):
    e = jnp.exp(-jnp.abs(x))
    u = 1.0 + e
    lg = jnp.where(u == 1.0, e, jnp.log(u) * e / (u - 1.0))
    return jnp.maximum(x, 0.0) + lg


def _row_spec(arr, tm):
    d = arr.shape[1]
    if arr.shape[0] == 1:
        return pl.BlockSpec((1, d), lambda i: (0, 0))
    return pl.BlockSpec((tm, d), lambda i: (i, 0))


def _const_spec(arr):
    nd = arr.ndim
    return pl.BlockSpec(arr.shape, lambda *_: (0,) * nd)


def _mod_kernel(c_ref, w_ref, b_ref, o_ref):
    s = _silu(c_ref[...]).astype(BF16)
    o_ref[...] = jnp.dot(s, w_ref[...].astype(BF16), preferred_element_type=F32) + b_ref[...]


def _ada_mod(c_all, w, b):
    r, d = c_all.shape
    n = w.shape[1]
    tn = 1024
    return pl.pallas_call(
        _mod_kernel,
        grid=(n // tn,),
        in_specs=[pl.BlockSpec((r, d), lambda j: (0, 0)),
                  pl.BlockSpec((d, tn), lambda j: (0, j)),
                  pl.BlockSpec((1, tn), lambda j: (0, j))],
        out_specs=pl.BlockSpec((r, tn), lambda j: (0, j)),
        out_shape=jax.ShapeDtypeStruct((r, n), F32),
        compiler_params=_params("arbitrary"),
        name="ada_mod",
    )(c_all, w, b.reshape(1, n))


def _ev_in_kernel(x_ref, sh_ref, sc_ref, w_ref, q_ref, z_ref, xbc_ref, kv_ref, dt_ref):
    x = x_ref[...]
    h = _rms(x) * (1.0 + sc_ref[...]) + sh_ref[...]
    y = jnp.dot(h.astype(BF16), w_ref[...], preferred_element_type=F32)
    q_ref[...] = y[:, 0:A_QW].astype(BF16)
    z_ref[...] = y[:, A_QW:A_QW + D_INNER]
    xbc_ref[...] = y[:, 1024:1024 + XBC_DIM]
    kv_ref[...] = y[:, 2048:2048 + 2 * A_KVW]
    dt_ref[...] = y[:, 2304:2304 + LANE]


def _ev_in(x, sh, sc, w):
    t, d = x.shape
    tm = _tile(t, 512)
    outs = [(A_QW, BF16), (D_INNER, F32), (XBC_DIM, F32), (2 * A_KVW, F32), (LANE, F32)]
    return pl.pallas_call(
        _ev_in_kernel,
        grid=(t // tm,),
        in_specs=[pl.BlockSpec((tm, d), lambda i: (i, 0)), _row_spec(sh, tm), _row_spec(sc, tm), _const_spec(w)],
        out_specs=[pl.BlockSpec((tm, n), lambda i: (i, 0)) for n, _ in outs],
        out_shape=[jax.ShapeDtypeStruct((t, n), dt) for n, dt in outs],
        compiler_params=_params("arbitrary"),
        name="ev_in_proj",
    )(x, sh, sc, w)


def _swa_heads(q, k_all, v_all, valid, distf, sink_ref):
    outs = []
    for g in range(A_KV_HEADS):
        k_g = k_all[:, A_HEAD_DIM * g:A_HEAD_DIM * (g + 1)].astype(BF16)
        v_g = v_all[:, A_HEAD_DIM * g:A_HEAD_DIM * (g + 1)].astype(BF16)
        for r in range(A_REP):
            h = g * A_REP + r
            q_h = q[:, A_HEAD_DIM * h:A_HEAD_DIM * (h + 1)]
            s = lax.dot_general(q_h, k_g, _NT, preferred_element_type=F32) * (A_HEAD_DIM ** -0.5)
            s = s - (2.0 ** (-8.0 * (h + 1) / A_HEADS)) * distf
            s = jnp.where(valid, s, -jnp.inf)
            sink = sink_ref[h]
            m = jnp.maximum(jnp.max(s, axis=-1, keepdims=True), sink)
            p = jnp.exp(s - m)
            den = jnp.sum(p, axis=-1, keepdims=True) + jnp.exp(sink - m)
            outs.append(jnp.dot(p.astype(BF16), v_g, preferred_element_type=F32) / den)
    return jnp.concatenate(outs, axis=-1)


def _swa_prompt_kernel(sink_ref, q_ref, kvp_ref, kvc_ref, o_ref):
    i = pl.program_id(0)
    kv = jnp.concatenate([kvp_ref[...], kvc_ref[...]], axis=0)
    row = lax.broadcasted_iota(jnp.int32, (WINDOW, 2 * WINDOW), 0)
    col = lax.broadcasted_iota(jnp.int32, (WINDOW, 2 * WINDOW), 1)
    dist = row + WINDOW - col
    first_key = jnp.where(i > 0, 0, WINDOW)
    valid = (dist >= 0) & (dist < WINDOW) & (col >= first_key)
    o = _swa_heads(q_ref[...], kv[:, :A_KVW], kv[:, A_KVW:], valid, dist.astype(F32), sink_ref)
    o_ref[...] = o.astype(BF16)


def _swa_prompt(q, kv, sinks):
    t = q.shape[0]
    nb = t // WINDOW
    return pl.pallas_call(
        _swa_prompt_kernel,
        grid=(nb,),
        in_specs=[pl.BlockSpec(memory_space=pltpu.SMEM),
                  pl.BlockSpec((WINDOW, A_QW), lambda i: (i, 0)),
                  pl.BlockSpec((WINDOW, 2 * A_KVW), lambda i: (jnp.maximum(i - 1, 0), 0)),
                  pl.BlockSpec((WINDOW, 2 * A_KVW), lambda i: (i, 0))],
        out_specs=pl.BlockSpec((WINDOW, A_QW), lambda i: (i, 0)),
        out_shape=jax.ShapeDtypeStruct((t, A_QW), BF16),
        compiler_params=_params("arbitrary"),
        name="swa_prompt",
    )(sinks, q, kv, kv)


def _swa_sample_kernel(sink_ref, q_ref, kvn_ref, buf_ref, o_ref, nbuf_ref, *, bb, l):
    w = WINDOW
    row = lax.broadcasted_iota(jnp.int32, (l, 2 * w), 0)
    col = lax.broadcasted_iota(jnp.int32, (l, 2 * w), 1)
    dist = row + w - col
    valid = (dist >= 0) & (dist < w)
    distf = dist.astype(F32)
    for b in range(bb):
        buf = buf_ref[b]
        kvn = kvn_ref[b]
        kv = jnp.concatenate([buf, kvn, jnp.zeros((w - l, 2 * A_KVW), F32)], axis=0)
        o = _swa_heads(q_ref[b], kv[:, :A_KVW], kv[:, A_KVW:], valid, distf, sink_ref)
        o_ref[b] = o.astype(BF16)
        nbuf_ref[b, 0:w - l, :] = buf[l:, :]
        nbuf_ref[b, w - l:w, :] = kvn


def _swa_sample(q, kvn, buf, sinks):
    b, l, _ = q.shape
    bb = _tile(b, 8)
    return pl.pallas_call(
        functools.partial(_swa_sample_kernel, bb=bb, l=l),
        grid=(b // bb,),
        in_specs=[pl.BlockSpec(memory_space=pltpu.SMEM),
                  pl.BlockSpec((bb, l, A_QW), lambda i: (i, 0, 0)),
                  pl.BlockSpec((bb, l, 2 * A_KVW), lambda i: (i, 0, 0)),
                  pl.BlockSpec((bb, WINDOW, 2 * A_KVW), lambda i: (i, 0, 0))],
        out_specs=[pl.BlockSpec((bb, l, A_QW), lambda i: (i, 0, 0)),
                   pl.BlockSpec((bb, WINDOW, 2 * A_KVW), lambda i: (i, 0, 0))],
        out_shape=[jax.ShapeDtypeStruct((b, l, A_QW), BF16),
                   jax.ShapeDtypeStruct((b, WINDOW, 2 * A_KVW), F32)],
        compiler_params=_params("arbitrary"),
        name="swa_sample",
    )(sinks, q, kvn, buf)


def _ssd_chunk(conv, z, dt, state, a_row, dsk, gn):
    L = SSD_CHUNK
    gw = D_INNER // B_GROUPS
    hpg = B_HEADS // B_GROUPS
    xs = conv[:, :D_INNER]
    adt = a_row * dt
    r2 = lax.broadcasted_iota(jnp.int32, (L, L), 0)
    c2 = lax.broadcasted_iota(jnp.int32, (L, L), 1)
    lower = r2 >= c2
    acs = jnp.dot(lower.astype(F32), adt, precision=_HI, preferred_element_type=F32)
    acs_t = acs.T
    band = lax.broadcasted_iota(jnp.int32, (L, gw), 1) // B_HEAD_DIM

    ys, states = [], []
    for g in range(B_GROUPS):
        b_g = conv[:, D_INNER + D_STATE * g:D_INNER + D_STATE * (g + 1)].astype(BF16)
        c_g = conv[:, D_INNER + B_GROUPS * D_STATE + D_STATE * g:
                   D_INNER + B_GROUPS * D_STATE + D_STATE * (g + 1)].astype(BF16)
        xs_g = xs[:, gw * g:gw * (g + 1)]

        def expand(mat, g=g):
            out = jnp.broadcast_to(mat[:, hpg * g:hpg * g + 1], (L, gw))
            for r in range(1, hpg):
                out = jnp.where(band == r, jnp.broadcast_to(mat[:, hpg * g + r:hpg * g + r + 1], (L, gw)), out)
            return out

        acs_e = expand(acs)
        xd = xs_g * expand(dt)
        xd_b = xd.astype(BF16)
        gmat = lax.dot_general(c_g, b_g, _NT, preferred_element_type=F32)
        y_diag = None
        for r in range(hpg):
            h = hpg * g + r
            diff = acs[:, h:h + 1] - acs_t[h:h + 1, :]
            lmat = jnp.exp(jnp.where(lower, diff, -jnp.inf))
            yr = jnp.dot((gmat * lmat).astype(BF16), xd_b, preferred_element_type=F32)
            y_diag = yr if y_diag is None else jnp.where(band == r, yr, y_diag)
        s_g = state[gw * g:gw * (g + 1), :]
        y_off = lax.dot_general(c_g, s_g.astype(BF16), _NT, preferred_element_type=F32) * jnp.exp(acs_e)
        decay = jnp.exp(acs_e[L - 1:L, :] - acs_e)
        upd = lax.dot_general((xd * decay).astype(BF16), b_g, _TN, preferred_element_type=F32)
        dec_rows = jnp.concatenate(
            [jnp.broadcast_to(jnp.exp(acs_t[hpg * g + r:hpg * g + r + 1, L - 1:L]), (B_HEAD_DIM, D_STATE))
             for r in range(hpg)], axis=0)
        states.append(s_g * dec_rows + upd)
        ys.append(y_diag + y_off + dsk[:, gw * g:gw * (g + 1)] * xs_g)

    outs = []
    for g in range(B_GROUPS):
        gt = ys[g] * _silu(z[:, gw * g:gw * (g + 1)])
        outs.append(_rms(gt))
    y = jnp.concatenate(outs, axis=-1) * gn
    return y, jnp.concatenate(states, axis=0)


def _conv_from_pad(xp_ref, cw_ref, cb_ref):
    L = SSD_CHUNK
    acc = cb_ref[...] + cw_ref[D_CONV - 1:D_CONV, :] * xp_ref[8:8 + L, :]
    for j in range(D_CONV - 1):
        acc = acc + cw_ref[j:j + 1, :] * xp_ref[5 + j:5 + j + L, :]
    return _silu(acc)


def _ssd_prompt_kernel(xbc_ref, z_ref, dt_ref, cw_ref, cb_ref, dtb_ref, a_ref, dsk_ref, gn_ref,
                       y_ref, st_ref, xp_ref, s_ref):
    i = pl.program_id(0)
    L = SSD_CHUNK

    @pl.when(i == 0)
    def _():
        xp_ref[0:8, :] = jnp.zeros((8, XBC_DIM), F32)
        s_ref[...] = jnp.zeros_like(s_ref)

    xp_ref[8:8 + L, :] = xbc_ref[...]
    conv = _conv_from_pad(xp_ref, cw_ref, cb_ref)
    xp_ref[0:8, :] = xp_ref[L:L + 8, :]
    dt = _softplus(dt_ref[...] + dtb_ref[...])
    y, new_state = _ssd_chunk(conv, z_ref[...], dt, s_ref[...], a_ref[...], dsk_ref[...], gn_ref[...])
    s_ref[...] = new_state
    y_ref[...] = y.astype(BF16)

    @pl.when(i == pl.num_programs(0) - 1)
    def _():
        st_ref[...] = new_state


def _ssd_prompt(xbc, z, dt, cw, cb, dtb, a_row, dsk, gn):
    t = xbc.shape[0]
    L = SSD_CHUNK
    consts = [cw, cb, dtb, a_row, dsk, gn]
    return pl.pallas_call(
        _ssd_prompt_kernel,
        grid=(t // L,),
        in_specs=[pl.BlockSpec((L, XBC_DIM), lambda i: (i, 0)),
                  pl.BlockSpec((L, D_INNER), lambda i: (i, 0)),
                  pl.BlockSpec((L, LANE), lambda i: (i, 0))] + [_const_spec(c) for c in consts],
        out_specs=[pl.BlockSpec((L, D_INNER), lambda i: (i, 0)),
                   pl.BlockSpec((B_HEADS * B_HEAD_DIM, D_STATE), lambda i: (0, 0))],
        out_shape=[jax.ShapeDtypeStruct((t, D_INNER), BF16),
                   jax.ShapeDtypeStruct((B_HEADS * B_HEAD_DIM, D_STATE), F32)],
        scratch_shapes=[pltpu.VMEM((L + 8, XBC_DIM), F32), pltpu.VMEM((B_HEADS * B_HEAD_DIM, D_STATE), F32)],
        compiler_params=_params("arbitrary"),
        name="ssd_prompt",
    )(xbc, z, dt, *consts)


def _ssd_sample_kernel(xbc_ref, z_ref, dt_ref, cs_ref, s0_ref, cw_ref, cb_ref, dtb_ref, a_ref, dsk_ref, gn_ref,
                       y_ref, st_ref, xp_ref, *, l):
    i = pl.program_id(0)
    L = SSD_CHUNK

    @pl.when(i == 0)
    def _():
        xp_ref[...] = jnp.zeros_like(xp_ref)

    xp_ref[5:8, :] = cs_ref[0]
    xp_ref[8:8 + l, :] = xbc_ref[0]
    conv = _conv_from_pad(xp_ref, cw_ref, cb_ref)
    pad = jnp.zeros((L - l, LANE), F32)
    dt = jnp.concatenate([_softplus(dt_ref[0] + dtb_ref[...]), pad], axis=0)
    z = jnp.concatenate([z_ref[0], jnp.zeros((L - l, D_INNER), F32)], axis=0)
    y, new_state = _ssd_chunk(conv, z, dt, s0_ref[0], a_ref[...], dsk_ref[...], gn_ref[...])
    y_ref[0] = y[0:l, :].astype(BF16)
    st_ref[0] = new_state


def _ssd_sample(xbc, z, dt, conv_state, ssm_state, cw, cb, dtb, a_row, dsk, gn):
    b, l, _ = xbc.shape
    L = SSD_CHUNK
    consts = [cw, cb, dtb, a_row, dsk, gn]
    hp = B_HEADS * B_HEAD_DIM
    return pl.pallas_call(
        functools.partial(_ssd_sample_kernel, l=l),
        grid=(b,),
        in_specs=[pl.BlockSpec((1, l, XBC_DIM), lambda i: (i, 0, 0)),
                  pl.BlockSpec((1, l, D_INNER), lambda i: (i, 0, 0)),
                  pl.BlockSpec((1, l, LANE), lambda i: (i, 0, 0)),
                  pl.BlockSpec((1, D_CONV - 1, XBC_DIM), lambda i: (i, 0, 0)),
                  pl.BlockSpec((1, hp, D_STATE), lambda i: (i, 0, 0))] + [_const_spec(c) for c in consts],
        out_specs=[pl.BlockSpec((1, l, D_INNER), lambda i: (i, 0, 0)),
                   pl.BlockSpec((1, hp, D_STATE), lambda i: (i, 0, 0))],
        out_shape=[jax.ShapeDtypeStruct((b, l, D_INNER), BF16),
                   jax.ShapeDtypeStruct((b, hp, D_STATE), F32)],
        scratch_shapes=[pltpu.VMEM((L + 8, XBC_DIM), F32)],
        compiler_params=_params("arbitrary"),
        name="ssd_sample",
    )(xbc, z, dt, conv_state, ssm_state, *consts)


def _proj_res_kernel(*refs, n_in):
    a_refs = refs[:n_in]
    x_ref, g_ref = refs[n_in:n_in + 2]
    w_refs = refs[n_in + 2:2 * n_in + 2]
    o_ref = refs[-1]
    acc = None
    for a_ref, w_ref in zip(a_refs, w_refs):
        part = jnp.dot(a_ref[...], w_ref[...], preferred_element_type=F32)
        acc = part if acc is None else acc + part
    o_ref[...] = x_ref[...] + g_ref[...] * acc


def _proj_res(acts, x, gate, ws):
    t, d = x.shape
    tm = _tile(t, 512)
    n_in = len(acts)
    return pl.pallas_call(
        functools.partial(_proj_res_kernel, n_in=n_in),
        grid=(t // tm,),
        in_specs=[pl.BlockSpec((tm, a.shape[1]), lambda i: (i, 0)) for a in acts]
        + [pl.BlockSpec((tm, d), lambda i: (i, 0)), _row_spec(gate, tm)] + [_const_spec(w) for w in ws],
        out_specs=pl.BlockSpec((tm, d), lambda i: (i, 0)),
        out_shape=jax.ShapeDtypeStruct((t, d), F32),
        compiler_params=_params("arbitrary"),
        name="proj_residual",
    )(*acts, x, gate, *ws)


def _swiglu_kernel(x_ref, sh_ref, sc_ref, g_ref, wgu_ref, wd_ref, o_ref, *, n_chunk):
    x = x_ref[...]
    h = (_rms(x) * (1.0 + sc_ref[...]) + sh_ref[...]).astype(BF16)
    f = wd_ref.shape[0]
    tf = f // n_chunk
    acc = None
    for c in range(n_chunk):
        gp = jnp.dot(h, wgu_ref[:, c * tf:(c + 1) * tf], preferred_element_type=F32)
        up = jnp.dot(h, wgu_ref[:, f + c * tf:f + (c + 1) * tf], preferred_element_type=F32)
        a = (_silu(gp) * up).astype(BF16)
        part = jnp.dot(a, wd_ref[c * tf:(c + 1) * tf, :], preferred_element_type=F32)
        acc = part if acc is None else acc + part
    o_ref[...] = x + g_ref[...] * acc


def _swiglu(x, sh, sc, gate, wgu, wd):
    t, d = x.shape
    tm = _tile(t, 512)
    return pl.pallas_call(
        functools.partial(_swiglu_kernel, n_chunk=2),
        grid=(t // tm,),
        in_specs=[pl.BlockSpec((tm, d), lambda i: (i, 0)), _row_spec(sh, tm), _row_spec(sc, tm), _row_spec(gate, tm),
                  _const_spec(wgu), _const_spec(wd)],
        out_specs=pl.BlockSpec((tm, d), lambda i: (i, 0)),
        out_shape=jax.ShapeDtypeStruct((t, d), F32),
        compiler_params=_params("arbitrary"),
        name="swiglu",
    )(x, sh, sc, gate, wgu, wd)


def _rope_lanes(blk, t_ref):
    half = C_ROPE // 2
    return (blk * t_ref[0] + pltpu.roll(blk, half, 1) * t_ref[1]
            + pltpu.roll(blk, LANE - half, 1) * t_ref[2])


def _od_latents(x_ref, sh_ref, sc_ref, w_in_ref, qn_ref, kvn_ref):
    x = x_ref[...]
    h = (_rms(x) * (1.0 + sc_ref[...]) + sh_ref[...]).astype(BF16)
    y = jnp.dot(h, w_in_ref[...], preferred_element_type=F32)
    cqn = (_rms(y[:, :Q_LORA]) * qn_ref[...]).astype(BF16)
    ckvn = _rms(y[:, Q_LORA:Q_LORA + KV_LORA]) * kvn_ref[...]
    return cqn, ckvn, y[:, Q_LORA + KV_LORA:]


def _od_in_prompt_kernel(x_ref, sh_ref, sc_ref, w_in_ref, qn_ref, kvn_ref, wuq_ref, wuk_ref, wuv_ref,
                         tq_ref, tk_ref, q_out, k_out, v_out, ckv_out, kpe_out):
    cqn, ckvn, kpe_pad = _od_latents(x_ref, sh_ref, sc_ref, w_in_ref, qn_ref, kvn_ref)
    ckv_out[...] = ckvn
    ckb = ckvn.astype(BF16)
    qf = jnp.dot(cqn, wuq_ref[...], preferred_element_type=F32)
    kf = jnp.dot(ckb, wuk_ref[...], preferred_element_type=F32)
    v_out[0] = lax.dot_general(wuv_ref[...], ckb, _NT, preferred_element_type=F32).astype(BF16)
    kpr = _rope_lanes(kpe_pad, tk_ref)
    kpe_out[...] = kpr[:, C_NOPE:C_NOPE + C_ROPE]
    for hh in range(C_HEADS):
        q_out[hh] = _rope_lanes(qf[:, LANE * hh:LANE * (hh + 1)], tq_ref).astype(BF16)
        k_out[hh] = (kf[:, LANE * hh:LANE * (hh + 1)] + kpr).astype(BF16)


def _od_in_prompt(x, sh, sc, w_in, qn, kvn, wuq, wuk, wuv, tq, tk):
    t, d = x.shape
    tm = _tile(t, MLA_TK)
    consts = [w_in, qn, kvn, wuq, wuk, wuv]
    return pl.pallas_call(
        _od_in_prompt_kernel,
        grid=(t // tm,),
        in_specs=[pl.BlockSpec((tm, d), lambda i: (i, 0)), _row_spec(sh, tm), _row_spec(sc, tm)]
        + [_const_spec(c) for c in consts]
        + [pl.BlockSpec((3, tm, LANE), lambda i: (0, i, 0)), pl.BlockSpec((3, tm, LANE), lambda i: (0, i, 0))],
        out_specs=[pl.BlockSpec((C_HEADS, tm, LANE), lambda i: (0, i, 0)),
                   pl.BlockSpec((C_HEADS, tm, LANE), lambda i: (0, i, 0)),
                   pl.BlockSpec((1, C_HEADS * C_V, tm), lambda i: (i, 0, 0)),
                   pl.BlockSpec((tm, KV_LORA), lambda i: (i, 0)),
                   pl.BlockSpec((tm, C_ROPE), lambda i: (i, 0))],
        out_shape=[jax.ShapeDtypeStruct((C_HEADS, t, LANE), BF16),
                   jax.ShapeDtypeStruct((C_HEADS, t, LANE), BF16),
                   jax.ShapeDtypeStruct((t // tm, C_HEADS * C_V, tm), BF16),
                   jax.ShapeDtypeStruct((t, KV_LORA), F32),
                   jax.ShapeDtypeStruct((t, C_ROPE), F32)],
        compiler_params=_params("arbitrary"),
        name="mla_proj_prompt",
    )(x, sh, sc, *consts, tq, tk)


def _od_in_sample_kernel(x_ref, sh_ref, sc_ref, w_in_ref, qn_ref, kvn_ref, wuq_ref, wukt_ref,
                         tq_ref, tk_ref, ql_out, qp_out, ckv_out, kpe_out, kpp_out):
    cqn, ckvn, kpe_pad = _od_latents(x_ref, sh_ref, sc_ref, w_in_ref, qn_ref, kvn_ref)
    ckv_out[...] = ckvn
    kpr = _rope_lanes(kpe_pad, tk_ref)
    kpe_out[...] = kpr[:, 0:C_ROPE]
    kpp_out[...] = kpr
    qf = jnp.dot(cqn, wuq_ref[...], preferred_element_type=F32)
    nq = C_HEADS * LANE
    for hh in range(C_HEADS):
        qn_h = (qf[:, LANE * hh:LANE * (hh + 1)] * MLA_SCALE).astype(BF16)
        ql = jnp.dot(qn_h, wukt_ref[hh], preferred_element_type=F32)
        ql_out[:, KV_LORA * hh:KV_LORA * (hh + 1)] = ql.astype(BF16)
        qp_out[:, LANE * hh:LANE * (hh + 1)] = _rope_lanes(qf[:, nq + LANE * hh:nq + LANE * (hh + 1)], tq_ref).astype(BF16)


def _od_in_sample(x, sh, sc, w_in, qn, kvn, wuq, wukt, tq, tk):
    t, d = x.shape
    tm = _tile(t, 512)
    consts = [w_in, qn, kvn, wuq, wukt]
    widths = [(C_HEADS * KV_LORA, BF16), (C_HEADS * LANE, BF16), (KV_LORA, F32), (C_ROPE, F32), (LANE, F32)]
    return pl.pallas_call(
        _od_in_sample_kernel,
        grid=(t // tm,),
        in_specs=[pl.BlockSpec((tm, d), lambda i: (i, 0)), _row_spec(sh, tm), _row_spec(sc, tm)]
        + [_const_spec(c) for c in consts]
        + [pl.BlockSpec((3, tm, LANE), lambda i: (0, i, 0)), pl.BlockSpec((3, tm, LANE), lambda i: (0, i, 0))],
        out_specs=[pl.BlockSpec((tm, n), lambda i: (i, 0)) for n, _ in widths],
        out_shape=[jax.ShapeDtypeStruct((t, n), dt) for n, dt in widths],
        compiler_params=_params("arbitrary"),
        name="mla_proj_sample",
    )(x, sh, sc, *consts, tq, tk)


def _flash_kernel(q_ref, k_ref, vt_ref, o_ref, m_ref, l_ref, acc_ref, sa_ref, mxa_ref, sb_ref, mxb_ref, *, tq, tk):
    qi = pl.program_id(1)
    m_ref[...] = jnp.full(m_ref.shape, -jnp.inf, F32)
    l_ref[...] = jnp.zeros(l_ref.shape, F32)
    acc_ref[...] = jnp.zeros(acc_ref.shape, F32)

    def scores(j, a, masked):
        start = pl.multiple_of(j * tk, tk)
        st = lax.dot_general(k_ref[a, pl.ds(start, tk), :], q_ref[a], _NT, preferred_element_type=F32)
        if masked:
            krow = lax.broadcasted_iota(jnp.int32, (tk, tq), 0) + (j * tk - qi * tq)
            qcol = lax.broadcasted_iota(jnp.int32, (tk, tq), 1)
            st = jnp.where(krow <= qcol, st, -jnp.inf)
        return st

    def absorb(j, a, st, mx):
        m_prev = m_ref[a]
        m_new = jnp.maximum(m_prev, mx)
        alpha = jnp.exp2(m_prev - m_new)
        p = jnp.exp2(st - m_new)
        l_ref[a] = alpha * l_ref[a] + jnp.sum(p, axis=0, keepdims=True)
        pv = jnp.dot(vt_ref[j, C_V * a:C_V * (a + 1), :], p.astype(BF16), preferred_element_type=F32)
        acc_ref[a] = alpha * acc_ref[a] + pv
        m_ref[a] = m_new

    n_full = (qi * tq) // tk
    for dd in range(max(tq // tk, 1)):
        for a in range(2):
            st = scores(n_full + dd, a, True)
            absorb(n_full + dd, a, st, jnp.max(st, axis=0, keepdims=True))

    def produce(j, buf):
        s_ref, mx_ref = buf
        for a in range(2):
            st = scores(j, a, False)
            s_ref[a] = st
            mx_ref[a] = jnp.max(st, axis=0, keepdims=True)

    def consume(j, buf):
        s_ref, mx_ref = buf
        for a in range(2):
            absorb(j, a, s_ref[a], mx_ref[a])

    buf_a, buf_b = (sa_ref, mxa_ref), (sb_ref, mxb_ref)

    @pl.when(n_full > 0)
    def _():
        produce(0, buf_a)
        n_loop = (n_full - 1) // 2

        def body(k, carry):
            produce(2 * k + 1, buf_b)
            consume(2 * k, buf_a)
            produce(2 * k + 2, buf_a)
            consume(2 * k + 1, buf_b)
            return carry

        lax.fori_loop(0, n_loop, body, 0)
        j0 = 2 * n_loop

        @pl.when(n_full - j0 == 2)
        def _():
            produce(j0 + 1, buf_b)
            consume(j0, buf_a)
            consume(j0 + 1, buf_b)

        @pl.when(n_full - j0 == 1)
        def _():
            consume(j0, buf_a)
    o_t = jnp.concatenate([acc_ref[0] / l_ref[0], acc_ref[1] / l_ref[1]], axis=0)
    o_ref[...] = o_t.T.astype(BF16)


def _flash_prompt(qh, kh, vt):
    nk, _, tk = vt.shape
    t = nk * tk
    tq = _tile(t, MLA_TQ)
    return pl.pallas_call(
        functools.partial(_flash_kernel, tq=tq, tk=tk),
        grid=(C_HEADS // 2, t // tq),
        in_specs=[pl.BlockSpec((2, tq, LANE), lambda hp, qi: (hp, qi, 0)),
                  pl.BlockSpec((2, t, LANE), lambda hp, qi: (hp, 0, 0)),
                  pl.BlockSpec((nk, 2 * C_V, tk), lambda hp, qi: (0, hp, 0))],
        out_specs=pl.BlockSpec((tq, LANE), lambda hp, qi: (qi, hp)),
        out_shape=jax.ShapeDtypeStruct((t, C_HEADS * C_V), BF16),
        scratch_shapes=[pltpu.VMEM((2, 1, tq), F32), pltpu.VMEM((2, 1, tq), F32), pltpu.VMEM((2, C_V, tq), F32),
                        pltpu.VMEM((2, tk, tq), F32), pltpu.VMEM((2, 1, tq), F32),
                        pltpu.VMEM((2, tk, tq), F32), pltpu.VMEM((2, 1, tq), F32)],
        compiler_params=_params("arbitrary", "arbitrary"),
        name="mla_flash_prompt",
    )(qh, kh, vt)


def _paged_kernel(pt_ref, ql_ref, qp_ref, cn_ref, kn_ref, ckv_hbm, kpt_hbm, o_ref,
                  ck_buf, kp_buf, ck_sem, kp_sem, m_ref, l_ref, acc_ref, *, ch, l):
    b = pl.program_id(0)
    nb = pl.num_programs(0)
    n_chunks = pt_ref.shape[1] // ch
    nq = ql_ref.shape[1]

    def ck_copy(bi, c, p, slot):
        page = pt_ref[bi, c * ch + p]
        return pltpu.make_async_copy(ckv_hbm.at[0, page], ck_buf.at[slot, pl.ds(p * PAGE_SIZE, PAGE_SIZE), :],
                                     ck_sem.at[slot])

    def kp_copy(bi, c, p, slot):
        page = pt_ref[bi, c * ch + p]
        return pltpu.make_async_copy(kpt_hbm.at[0, page], kp_buf.at[slot, 0:C_ROPE, pl.ds(p * PAGE_SIZE, PAGE_SIZE)],
                                     kp_sem.at[slot])

    def start_chunk(bi, c, slot):
        for p in range(ch):
            ck_copy(bi, c, p, slot).start()
            kp_copy(bi, c, p, slot).start()

    def wait_chunk(slot):
        for p in range(ch):
            ck_copy(0, 0, p, slot).wait()
            kp_copy(0, 0, p, slot).wait()

    @pl.when(b == 0)
    def _():
        kp_buf[...] = jnp.zeros_like(kp_buf)
        start_chunk(0, 0, 0)

    m_ref[...] = jnp.full(m_ref.shape, -jnp.inf, F32)
    l_ref[...] = jnp.zeros(l_ref.shape, F32)
    acc_ref[...] = jnp.zeros(acc_ref.shape, F32)
    ql = ql_ref[0]
    qp = qp_ref[0]

    def update(k, s, values):
        m_prev = m_ref[k]
        m_new = jnp.maximum(m_prev, jnp.max(s, axis=-1, keepdims=True))
        alpha = jnp.exp(m_prev - m_new)
        p = jnp.exp(s - m_new)
        l_ref[k] = alpha * l_ref[k] + jnp.sum(p, axis=-1, keepdims=True)
        acc_ref[k] = alpha * acc_ref[k] + jnp.dot(p.astype(BF16), values, preferred_element_type=F32)
        m_ref[k] = m_new

    n_split = m_ref.shape[0]
    span = ch * PAGE_SIZE // n_split

    def step(c, slot):
        @pl.when(c + 1 < n_chunks)
        def _():
            start_chunk(b, c + 1, 1 - slot)

        @pl.when((c + 1 == n_chunks) & (b + 1 < nb))
        def _():
            start_chunk(b + 1, 0, 1 - slot)

        wait_chunk(slot)
        for k in range(n_split):
            ck = ck_buf[slot, k * span:(k + 1) * span, :].astype(BF16)
            kp = kp_buf[slot, :, k * span:(k + 1) * span].astype(BF16)
            s = (lax.dot_general(ql, ck, _NT, preferred_element_type=F32)
                 + jnp.dot(qp, kp, preferred_element_type=F32))
            update(k, s, ck)

    def pair(c2, carry):
        step(2 * c2, 0)
        step(2 * c2 + 1, 1)
        return carry

    lax.fori_loop(0, n_chunks // 2, pair, 0)

    cn = jnp.concatenate([cn_ref[0], jnp.zeros((LANE - l, KV_LORA), F32)], axis=0).astype(BF16)
    kn_t = jnp.concatenate([kn_ref[0], jnp.zeros((LANE - l, LANE), F32)], axis=0).T.astype(BF16)
    s = lax.dot_general(ql, cn, _NT, preferred_element_type=F32) + jnp.dot(qp, kn_t, preferred_element_type=F32)
    qtok = lax.broadcasted_iota(jnp.int32, (nq, LANE), 0) // C_HEADS
    kcol = lax.broadcasted_iota(jnp.int32, (nq, LANE), 1)
    update(0, jnp.where(kcol <= qtok, s, -jnp.inf), cn)
    m_all = m_ref[0]
    for k in range(1, n_split):
        m_all = jnp.maximum(m_all, m_ref[k])
    num = jnp.zeros((nq, KV_LORA), F32)
    den = jnp.zeros((nq, 1), F32)
    for k in range(n_split):
        w = jnp.exp(m_ref[k] - m_all)
        num = num + w * acc_ref[k]
        den = den + w * l_ref[k]
    o_ref[0] = (num / den).astype(BF16)


def _paged_attention(page_table, ql, qp, ckv_new, kpe_new_pad, cache_ckv, cache_kpe_t):
    b, nq, _ = ql.shape
    l = ckv_new.shape[1]
    n_pages = page_table.shape[1]
    ch = max(c for c in (16, 8, 4, 2, 1) if n_pages % (2 * c) == 0)
    grid_spec = pltpu.PrefetchScalarGridSpec(
        num_scalar_prefetch=1,
        grid=(b,),
        in_specs=[pl.BlockSpec((1, nq, KV_LORA), lambda bi, pt: (bi, 0, 0)),
                  pl.BlockSpec((1, nq, LANE), lambda bi, pt: (bi, 0, 0)),
                  pl.BlockSpec((1, l, KV_LORA), lambda bi, pt: (bi, 0, 0)),
                  pl.BlockSpec((1, l, LANE), lambda bi, pt: (bi, 0, 0)),
                  pl.BlockSpec(memory_space=pl.ANY),
                  pl.BlockSpec(memory_space=pl.ANY)],
        out_specs=pl.BlockSpec((1, nq, KV_LORA), lambda bi, pt: (bi, 0, 0)),
        scratch_shapes=[pltpu.VMEM((2, ch * PAGE_SIZE, KV_LORA), F32),
                        pltpu.VMEM((2, LANE, ch * PAGE_SIZE), F32),
                        pltpu.SemaphoreType.DMA((2,)),
                        pltpu.SemaphoreType.DMA((2,)),
                        pltpu.VMEM((PAGED_STREAMS, nq, 1), F32), pltpu.VMEM((PAGED_STREAMS, nq, 1), F32),
                        pltpu.VMEM((PAGED_STREAMS, nq, KV_LORA), F32)],
    )
    return pl.pallas_call(
        functools.partial(_paged_kernel, ch=ch, l=l),
        grid_spec=grid_spec,
        out_shape=jax.ShapeDtypeStruct((b, nq, KV_LORA), BF16),
        compiler_params=_params("arbitrary"),
        name="mla_paged_sample",
    )(page_table, ql, qp, ckv_new, kpe_new_pad, cache_ckv, cache_kpe_t)


def _mla_out_sample_kernel(o_ref, x_ref, g_ref, wuv_ref, wo_ref, out_ref):
    parts = []
    for pr in range(C_HEADS // 2):
        parts.append(jnp.dot(o_ref[:, 2 * KV_LORA * pr:2 * KV_LORA * (pr + 1)], wuv_ref[pr],
                             preferred_element_type=F32))
    attn = jnp.concatenate(parts, axis=-1).astype(BF16)
    out_ref[...] = x_ref[...] + g_ref[...] * jnp.dot(attn, wo_ref[...], preferred_element_type=F32)


def _mla_out_sample(o_lat, x, gate, wuv_pairs, wo):
    t, d = x.shape
    tm = _tile(t, 512)
    return pl.pallas_call(
        _mla_out_sample_kernel,
        grid=(t // tm,),
        in_specs=[pl.BlockSpec((tm, o_lat.shape[1]), lambda i: (i, 0)), pl.BlockSpec((tm, d), lambda i: (i, 0)),
                  _row_spec(gate, tm), _const_spec(wuv_pairs), _const_spec(wo)],
        out_specs=pl.BlockSpec((tm, d), lambda i: (i, 0)),
        out_shape=jax.ShapeDtypeStruct((t, d), F32),
        compiler_params=_params("arbitrary"),
        name="mla_out_sample",
    )(o_lat, x, gate, wuv_pairs, wo)


def _route_kernel(x_ref, sh_ref, sc_ref, r_ref, h_ref, gate_ref, idx_ref):
    x = x_ref[...]
    h = _rms(x) * (1.0 + sc_ref[...]) + sh_ref[...]
    h_ref[...] = h
    logits = jnp.dot(h, r_ref[...], precision=_HI, preferred_element_type=F32)
    lane = lax.broadcasted_iota(jnp.int32, logits.shape, 1)
    logits = jnp.where(lane < N_EXPERTS, logits, -jnp.inf)
    v1 = jnp.max(logits, axis=-1, keepdims=True)
    i1 = jnp.min(jnp.where(logits == v1, lane, LANE), axis=-1, keepdims=True)
    rest = jnp.where(lane == i1, -jnp.inf, logits)
    v2 = jnp.max(rest, axis=-1, keepdims=True)
    i2 = jnp.min(jnp.where(rest == v2, lane, LANE), axis=-1, keepdims=True)
    e2 = jnp.exp(v2 - v1)
    gate_ref[...] = jnp.where(lane == 0, 1.0 / (1.0 + e2), jnp.where(lane == 1, e2 / (1.0 + e2), 0.0))
    idx_ref[...] = jnp.where(lane == 0, i1, jnp.where(lane == 1, i2, 0))


def _route(x, sh, sc, router_pad):
    t, d = x.shape
    tm = _tile(t, 512)
    return pl.pallas_call(
        _route_kernel,
        grid=(t // tm,),
        in_specs=[pl.BlockSpec((tm, d), lambda i: (i, 0)), _row_spec(sh, tm), _row_spec(sc, tm), _const_spec(router_pad)],
        out_specs=[pl.BlockSpec((tm, d), lambda i: (i, 0)), pl.BlockSpec((tm, LANE), lambda i: (i, 0)),
                   pl.BlockSpec((tm, LANE), lambda i: (i, 0))],
        out_shape=[jax.ShapeDtypeStruct((t, d), F32), jax.ShapeDtypeStruct((t, LANE), F32),
                   jax.ShapeDtypeStruct((t, LANE), jnp.int32)],
        compiler_params=_params("arbitrary"),
        name="moe_route",
    )(x, sh, sc, router_pad)


def _moe_plan(idx, tm):
    t = idx.shape[0]
    n_tiles = -(-2 * t // tm) + N_EXPERTS
    e_flat = idx.reshape(-1)
    onehot = (e_flat[:, None] == jnp.arange(N_EXPERTS, dtype=jnp.int32)[None]).astype(jnp.int32)
    csum = jnp.cumsum(onehot, axis=0)
    rank = jnp.sum(csum * onehot, axis=1) - 1
    padded = (csum[-1] + tm - 1) // tm * tm
    gend = jnp.cumsum(padded)
    dest = (gend - padded)[e_flat] + rank
    row_token = jnp.zeros((n_tiles * tm,), jnp.int32).at[dest].set(jnp.arange(2 * t, dtype=jnp.int32) // 2)
    n_used = gend[-1] // tm
    tile_start = jnp.minimum(jnp.arange(n_tiles, dtype=jnp.int32), n_used - 1) * tm
    tile_expert = jnp.sum((gend[None, :] <= tile_start[:, None]).astype(jnp.int32), axis=1)
    return dest, row_token, tile_expert.astype(jnp.int32), n_used.reshape(1).astype(jnp.int32)


def _row_loop(n, fn):
    def body(r, carry):
        fn(r)
        return carry

    lax.fori_loop(0, n, body, 0, unroll=8)


def _moe_gather_kernel(rt_ref, h_hbm, xs_hbm, sem, *, tm):
    i = pl.program_id(0)

    def row_copy(row, tok):
        return pltpu.make_async_copy(h_hbm.at[pl.ds(tok, 1)], xs_hbm.at[pl.ds(row, 1)], sem.at[0])

    _row_loop(tm, lambda r: row_copy(i * tm + r, rt_ref[i * tm + r]).start())

    @pl.when(i > 0)
    def _():
        _row_loop(tm, lambda r: row_copy(0, 0).wait())

    @pl.when(i == pl.num_programs(0) - 1)
    def _():
        _row_loop(tm, lambda r: row_copy(0, 0).wait())


def _moe_gather(row_token, h, tm):
    r = row_token.shape[0]
    d = h.shape[1]
    return pl.pallas_call(
        functools.partial(_moe_gather_kernel, tm=tm),
        grid_spec=pltpu.PrefetchScalarGridSpec(
            num_scalar_prefetch=1, grid=(r // tm,),
            in_specs=[pl.BlockSpec(memory_space=pl.ANY)],
            out_specs=pl.BlockSpec(memory_space=pl.ANY),
            scratch_shapes=[pltpu.SemaphoreType.DMA((1,))]),
        out_shape=jax.ShapeDtypeStruct((r, d), h.dtype),
        compiler_params=_params("arbitrary"),
        name="moe_gather",
    )(row_token, h)


def _moe_grouped_kernel(te_ref, nu_ref, xs_ref, wg_ref, wu_ref, wd_ref, y_ref, xb_ref):
    del te_ref
    i = pl.program_id(0)
    c = pl.program_id(1)

    @pl.when(c == 0)
    def _():
        xb_ref[...] = xs_ref[...].astype(BF16)
        y_ref[...] = jnp.zeros_like(y_ref)

    @pl.when(i < nu_ref[0])
    def _():
        xb = xb_ref[...]
        gp = jnp.dot(xb, wg_ref[0], preferred_element_type=F32)
        up = jnp.dot(xb, wu_ref[0], preferred_element_type=F32)
        y_ref[...] += jnp.dot((_silu(gp) * up).astype(BF16), wd_ref[0], preferred_element_type=F32)


def _moe_grouped(tile_expert, n_used, xs, wgu, wd, tm):
    r, d = xs.shape
    f = wd.shape[1]
    tf = 512
    nc = f // tf

    def chunk(i, c, nu):
        return jnp.where(i < nu[0], c, nc - 1)

    return pl.pallas_call(
        _moe_grouped_kernel,
        grid_spec=pltpu.PrefetchScalarGridSpec(
            num_scalar_prefetch=2, grid=(r // tm, nc),
            in_specs=[pl.BlockSpec((tm, d), lambda i, c, te, nu: (i, 0)),
                      pl.BlockSpec((1, d, tf), lambda i, c, te, nu: (te[i], 0, chunk(i, c, nu))),
                      pl.BlockSpec((1, d, tf), lambda i, c, te, nu: (te[i], 0, nc + chunk(i, c, nu))),
                      pl.BlockSpec((1, tf, d), lambda i, c, te, nu: (te[i], chunk(i, c, nu), 0))],
            out_specs=pl.BlockSpec((tm, d), lambda i, c, te, nu: (i, 0)),
            scratch_shapes=[pltpu.VMEM((tm, d), BF16)]),
        out_shape=jax.ShapeDtypeStruct((r, d), F32),
        compiler_params=_params("arbitrary", "arbitrary"),
        name="moe_grouped",
    )(tile_expert, n_used, xs, wgu, wgu, wd)


def _moe_combine_kernel(pos_ref, x_ref, g2_ref, gate_ref, fn_ref, y_hbm, o_ref, ya_ref, yb_ref, sem, *, tm):
    i = pl.program_id(0)
    slot = i % 2

    def row_copy(buf, s, r, src):
        return pltpu.make_async_copy(y_hbm.at[pl.ds(src, 1)], buf.at[s, pl.ds(r, 1)], sem.at[s])

    def issue(step, s):
        def one(r):
            t = step * tm + r
            row_copy(ya_ref, s, r, pos_ref[2 * t]).start()
            row_copy(yb_ref, s, r, pos_ref[2 * t + 1]).start()

        _row_loop(tm, one)

    @pl.when(i == 0)
    def _():
        issue(0, 0)

    @pl.when(i + 1 < pl.num_programs(0))
    def _():
        issue(i + 1, 1 - slot)

    def wait_one(r):
        row_copy(ya_ref, slot, 0, 0).wait()
        row_copy(yb_ref, slot, 0, 0).wait()

    _row_loop(tm, wait_one)
    gate = gate_ref[...]
    y = gate[:, 0:1] * ya_ref[slot] + gate[:, 1:2] * yb_ref[slot]
    xo = x_ref[...] + g2_ref[...] * y
    o_ref[...] = _rms(xo) * fn_ref[...]


def _moe_combine(pos, x, g2, gate, fn, y):
    t, d = x.shape
    tm = _tile(t, 256)
    return pl.pallas_call(
        functools.partial(_moe_combine_kernel, tm=tm),
        grid_spec=pltpu.PrefetchScalarGridSpec(
            num_scalar_prefetch=1, grid=(t // tm,),
            in_specs=[pl.BlockSpec((tm, d), lambda i, p: (i, 0)),
                      (pl.BlockSpec((1, d), lambda i, p: (0, 0)) if g2.shape[0] == 1
                       else pl.BlockSpec((tm, d), lambda i, p: (i, 0))),
                      pl.BlockSpec((tm, LANE), lambda i, p: (i, 0)),
                      pl.BlockSpec((1, d), lambda i, p: (0, 0)),
                      pl.BlockSpec(memory_space=pl.ANY)],
            out_specs=pl.BlockSpec((tm, d), lambda i, p: (i, 0)),
            scratch_shapes=[pltpu.VMEM((2, tm, d), F32), pltpu.VMEM((2, tm, d), F32), pltpu.SemaphoreType.DMA((2,))]),
        out_shape=jax.ShapeDtypeStruct((t, d), F32),
        compiler_params=_params("arbitrary"),
        name="moe_combine",
    )(pos, x, g2, gate, fn, y)


def _rope_tables(pos, lo, reps, scale):
    half = C_ROPE // 2
    freqs = ROPE_THETA ** (-2.0 * jnp.arange(half, dtype=F32) / C_ROPE)
    ang = pos.astype(F32)[:, None] * freqs[None]
    cos, sin = jnp.cos(ang), jnp.sin(ang)
    n = pos.shape[0]
    zeros = jnp.zeros((n, half), F32)
    c_grp = jnp.concatenate([cos, cos], axis=1)
    s1_grp = jnp.concatenate([zeros, sin], axis=1)
    s2_grp = jnp.concatenate([-sin, zeros], axis=1)

    def lay(grp, fill):
        body = jnp.tile(grp, (1, reps))
        left = jnp.full((n, lo), fill, F32)
        right = jnp.zeros((n, LANE - lo - reps * C_ROPE), F32)
        return jnp.concatenate([left, body, right], axis=1)

    return jnp.stack([lay(c_grp, 1.0), lay(s1_grp, 0.0), lay(s2_grp, 0.0)]) * scale


def _pad_cols(w, n):
    return jnp.pad(w, ((0, 0), (0, n - w.shape[1])))


def kernel(x_prompt, x_sample, c_prompt, c_sample, state_swa_kv, state_conv, state_ssm, cache_ckv, cache_kpe, page_table,
           ev_mod_w, ev_mod_b, ev_w_in, ev_sinks, ev_conv_w, ev_conv_b, ev_dt_bias, ev_a_log, ev_d_skip, ev_gnorm,
           ev_w_out, ev_w_gu, ev_w_down, od_mod_w, od_mod_b, od_w_in, od_qnorm, od_kvnorm, od_w_uq, od_w_uk, od_w_uv,
           od_w_out, od_router, od_w_gu, od_w_down, final_norm):
    d = D_MODEL
    _, tp, _ = x_prompt.shape
    bs, ls, _ = x_sample.shape
    ts = bs * ls
    n_pages = page_table.shape[1]
    past = n_pages * PAGE_SIZE
    assert state_swa_kv.shape[2] == WINDOW and ls <= 8 and ls >= D_CONV - 1

    xp = x_prompt.reshape(tp, d)
    xs = x_sample.reshape(ts, d)

    n_c = 1 + bs
    n_cp = -(-n_c // 8) * 8
    c_all = jnp.pad(jnp.concatenate([c_prompt, c_sample], axis=0), ((0, n_cp - n_c), (0, 0)))

    def mods(w, b):
        m = _ada_mod(c_all, w, b)
        mp = [m[0:1, k * d:(k + 1) * d] for k in range(MOD_SLOTS)]
        ms = [jnp.repeat(m[1:n_c, k * d:(k + 1) * d], ls, axis=0) for k in range(MOD_SLOTS)]
        return mp, ms

    i = 0
    mp, ms = mods(ev_mod_w[i], ev_mod_b[i])
    w_in = ev_w_in[i]
    o_k, o_v, o_z, o_x, o_dt = A_QW, A_QW + A_KVW, A_QW + 2 * A_KVW, A_QW + 2 * A_KVW + D_INNER, A_QW + 2 * A_KVW + D_INNER + XBC_DIM
    w_in_p = jnp.concatenate([w_in[:, :o_k], w_in[:, o_z:o_x], w_in[:, o_x:o_dt], w_in[:, o_k:o_z],
                              _pad_cols(w_in[:, o_dt:], LANE)], axis=1).astype(BF16)
    sinks = ev_sinks[i].astype(F32)
    cw = ev_conv_w[i]
    cb = ev_conv_b[i].reshape(1, XBC_DIM)
    dtb = _pad_cols(ev_dt_bias[i].reshape(1, B_HEADS).astype(F32), LANE)
    a_row = _pad_cols(-jnp.exp(ev_a_log[i].astype(F32)).reshape(1, B_HEADS), LANE)
    dsk = jnp.repeat(ev_d_skip[i].astype(F32), B_HEAD_DIM).reshape(1, D_INNER)
    gn = ev_gnorm[i].reshape(1, D_INNER)
    w_out = ev_w_out[i].astype(BF16)
    w_gu = ev_w_gu[i].astype(BF16)
    w_dn = ev_w_down[i].astype(BF16)
    ssd_consts = (cw, cb, dtb, a_row, dsk, gn)

    q, z, xbc, kv, dt = _ev_in(xp, mp[0], mp[1], w_in_p)
    attn = _swa_prompt(q, kv, sinks)
    ssm, ssm_state_p = _ssd_prompt(xbc, z, dt, *ssd_consts)
    xp = _proj_res([attn, ssm], xp, mp[2], [w_out[:A_QW], w_out[A_QW:]])
    xp = _swiglu(xp, mp[3], mp[4], mp[5], w_gu, w_dn)
    swa_kv_prompt = kv[tp - WINDOW:].reshape(1, 1, WINDOW, 2, A_KV_HEADS, A_HEAD_DIM)
    conv_prompt = xbc[tp - (D_CONV - 1):].reshape(1, 1, D_CONV - 1, XBC_DIM)
    ssm_prompt = ssm_state_p.reshape(1, 1, B_HEADS, B_HEAD_DIM, D_STATE)

    q, z, xbc, kv, dt = _ev_in(xs, ms[0], ms[1], w_in_p)
    buf = state_swa_kv[i].reshape(bs, WINDOW, 2 * A_KVW)
    attn, nbuf = _swa_sample(q.reshape(bs, ls, A_QW), kv.reshape(bs, ls, 2 * A_KVW), buf, sinks)
    xbc3 = xbc.reshape(bs, ls, XBC_DIM)
    ssm, ssm_state_s = _ssd_sample(xbc3, z.reshape(bs, ls, D_INNER), dt.reshape(bs, ls, LANE), state_conv[i],
                                   state_ssm[i].reshape(bs, B_HEADS * B_HEAD_DIM, D_STATE), *ssd_consts)
    xs = _proj_res([attn.reshape(ts, A_QW), ssm.reshape(ts, D_INNER)], xs, ms[2], [w_out[:A_QW], w_out[A_QW:]])
    xs = _swiglu(xs, ms[3], ms[4], ms[5], w_gu, w_dn)
    swa_kv_sample = nbuf.reshape(1, bs, WINDOW, 2, A_KV_HEADS, A_HEAD_DIM)
    conv_sample = xbc3[:, ls - (D_CONV - 1):].reshape(1, bs, D_CONV - 1, XBC_DIM)
    ssm_sample = ssm_state_s.reshape(1, bs, B_HEADS, B_HEAD_DIM, D_STATE)

    mp, ms = mods(od_mod_w[i], od_mod_b[i])
    w_in = od_w_in[i]
    w_cq, w_ckv, w_kpe = w_in[:, :Q_LORA], w_in[:, Q_LORA:Q_LORA + KV_LORA], w_in[:, Q_LORA + KV_LORA:]
    qn = od_qnorm[i].reshape(1, Q_LORA)
    kvn = od_kvnorm[i].reshape(1, KV_LORA)
    w_uq = od_w_uq[i].reshape(Q_LORA, C_HEADS, C_NOPE + C_ROPE)
    w_uk = od_w_uk[i]
    w_uv = od_w_uv[i]
    w_o = od_w_out[i].astype(BF16)

    w_in_pp = jnp.concatenate([w_cq, w_ckv, jnp.pad(w_kpe, ((0, 0), (C_NOPE, LANE - C_NOPE - C_ROPE)))], axis=1).astype(BF16)
    wuq_p = jnp.pad(w_uq, ((0, 0), (0, 0), (0, LANE - C_NOPE - C_ROPE))).reshape(Q_LORA, C_HEADS * LANE).astype(BF16)
    wuk_p = jnp.pad(w_uk, ((0, 0), (0, 0), (0, LANE - C_NOPE))).reshape(KV_LORA, C_HEADS * LANE).astype(BF16)
    wuv_p = w_uv.reshape(KV_LORA, C_HEADS * C_V).T.astype(BF16)
    pos_p = jnp.arange(tp, dtype=jnp.int32)
    tq_p = _rope_tables(pos_p, C_NOPE, 1, MLA_SCALE * LOG2E)
    tk_p = _rope_tables(pos_p, C_NOPE, 1, 1.0)
    qh, kh, v, ckv_p, kpe_p = _od_in_prompt(xp, mp[0], mp[1], w_in_pp, qn, kvn, wuq_p, wuk_p, wuv_p, tq_p, tk_p)
    attn = _flash_prompt(qh, kh, v)
    xp = _proj_res([attn], xp, mp[2], [w_o])

    w_in_ps = jnp.concatenate([w_cq, w_ckv, _pad_cols(w_kpe, LANE)], axis=1).astype(BF16)
    wq_nope = jnp.pad(w_uq[:, :, :C_NOPE], ((0, 0), (0, 0), (0, LANE - C_NOPE))).reshape(Q_LORA, C_HEADS * LANE)
    wq_rope = jnp.pad(w_uq[:, :, C_NOPE:], ((0, 0), (0, 0), (0, LANE - C_ROPE))).reshape(Q_LORA, C_HEADS * LANE)
    wuq_s = jnp.concatenate([wq_nope, wq_rope], axis=1).astype(BF16)
    wukt_s = jnp.pad(jnp.transpose(w_uk, (1, 2, 0)), ((0, 0), (0, LANE - C_NOPE), (0, 0))).astype(BF16)
    pos_s = jnp.tile(past + jnp.arange(ls, dtype=jnp.int32), bs)
    tq_s = _rope_tables(pos_s, 0, 1, MLA_SCALE)
    tk_s = _rope_tables(pos_s, 0, 1, 1.0)
    ql, qp, ckv_s, kpe_s, kpe_pad_s = _od_in_sample(xs, ms[0], ms[1], w_in_ps, qn, kvn, wuq_s, wukt_s, tq_s, tk_s)
    nq = ls * C_HEADS
    o_lat = _paged_attention(page_table, ql.reshape(bs, nq, KV_LORA), qp.reshape(bs, nq, LANE),
                             ckv_s.reshape(bs, ls, KV_LORA), kpe_pad_s.reshape(bs, ls, LANE),
                             cache_ckv[i:i + 1], jnp.swapaxes(cache_kpe[i:i + 1], 2, 3))
    wuv_h = jnp.transpose(w_uv, (1, 0, 2)).reshape(C_HEADS // 2, 2, KV_LORA, C_V)
    zero = jnp.zeros((C_HEADS // 2, KV_LORA, C_V), F32)
    wuv_pairs = jnp.concatenate([jnp.concatenate([wuv_h[:, 0], zero], axis=2),
                                 jnp.concatenate([zero, wuv_h[:, 1]], axis=2)], axis=1).astype(BF16)
    xs = _mla_out_sample(o_lat.reshape(ts, C_HEADS * KV_LORA), xs, ms[2], wuv_pairs, w_o)

    router_pad = _pad_cols(od_router[i].astype(F32), LANE)
    wgu_e = od_w_gu[i].astype(BF16)
    wdn_e = od_w_down[i].astype(BF16)
    fn = final_norm.reshape(1, d).astype(F32)
    h_p, gate_p, idx_p = _route(xp, mp[3], mp[4], router_pad)
    h_s, gate_s, idx_s = _route(xs, ms[3], ms[4], router_pad)
    idx_all = jnp.concatenate([idx_p[:, :2], idx_s[:, :2]], axis=0)
    dest, row_token, tile_expert, n_used = _moe_plan(idx_all, MOE_TM)
    xs_sorted = _moe_gather(row_token, jnp.concatenate([h_p, h_s], axis=0), MOE_TM)
    y_sorted = _moe_grouped(tile_expert, n_used, xs_sorted, wgu_e, wdn_e, MOE_TM)
    y_prompt = _moe_combine(dest[:2 * tp], xp, mp[5], gate_p, fn, y_sorted)
    y_sample = _moe_combine(dest[2 * tp:], xs, ms[5], gate_s, fn, y_sorted)

    return (y_prompt.reshape(1, tp, d), y_sample.reshape(bs, ls, d),
            swa_kv_prompt, swa_kv_sample, conv_prompt, conv_sample, ssm_prompt, ssm_sample,
            ckv_p.reshape(1, 1, tp, KV_LORA), ckv_s.reshape(1, bs, ls, KV_LORA),
            kpe_p.reshape(1, 1, tp, C_ROPE), kpe_s.reshape(1, bs, ls, C_ROPE))
```

```python
import functools
import math

import jax
import jax.numpy as jnp
from jax import lax
from jax.experimental import pallas as pl
from jax.experimental.pallas import tpu as pltpu

F32 = jnp.float32
BF16 = jnp.bfloat16

D_MODEL = 1024
EPS = 1e-6
MOD_SLOTS = 6

A_HEAD_DIM = 64
A_HEADS = 8
A_KV_HEADS = 2
A_REP = A_HEADS // A_KV_HEADS
WINDOW = 128
A_QW = A_HEADS * A_HEAD_DIM
A_KVW = A_KV_HEADS * A_HEAD_DIM

D_INNER = 512
B_HEAD_DIM = 64
B_HEADS = 8
B_GROUPS = 2
D_STATE = 128
D_CONV = 4
SSD_CHUNK = 128
XBC_DIM = D_INNER + 2 * B_GROUPS * D_STATE

C_HEADS = 16
C_NOPE = 64
C_ROPE = 32
C_V = 64
Q_LORA = 384
KV_LORA = 256
ROPE_THETA = 10000.0
MLA_SCALE = (C_NOPE + C_ROPE) ** -0.5
PAGE_SIZE = 128

D_FF = 2816
N_EXPERTS = 8
D_FF_EXPERT = 3584

LANE = 128
MLA_TK = 512
MLA_TQ = 512
LOG2E = 1.4426950408889634
MOE_TM = 512
VMEM_LIMIT = 56 * 1024 * 1024

_HI = lax.Precision.HIGHEST
_NT = (((1,), (1,)), ((), ()))
_TN = (((0,), (0,)), ((), ()))


def _params(*sem):
    return pltpu.CompilerParams(dimension_semantics=sem, vmem_limit_bytes=VMEM_LIMIT)


def _tile(n, pref):
    t = min(n, pref)
    while n % t:
        t -= 8
    return t


def _rms(x):
    return x * lax.rsqrt(jnp.mean(x * x, axis=-1, keepdims=True) + EPS)


def _silu(x):
    return x * jax.nn.sigmoid(x)


def _softplus(x):
    e = jnp.exp(-jnp.abs(x))
    u = 1.0 + e
    lg = jnp.where(u == 1.0, e, jnp.log(u) * e / (u - 1.0))
    return jnp.maximum(x, 0.0) + lg


def _row_spec(arr, tm):
    d = arr.shape[1]
    if arr.shape[0] == 1:
        return pl.BlockSpec((1, d), lambda i: (0, 0))
    return pl.BlockSpec((tm, d), lambda i: (i, 0))


def _const_spec(arr):
    nd = arr.ndim
    return pl.BlockSpec(arr.shape, lambda *_: (0,) * nd)


def _mod_kernel(c_ref, w_ref, b_ref, o_ref):
    s = _silu(c_ref[...]).astype(BF16)
    o_ref[...] = jnp.dot(s, w_ref[...].astype(BF16), preferred_element_type=F32) + b_ref[...]


def _ada_mod(c_all, w, b):
    r, d = c_all.shape
    n = w.shape[1]
    tn = 1024
    return pl.pallas_call(
        _mod_kernel,
        grid=(n // tn,),
        in_specs=[pl.BlockSpec((r, d), lambda j: (0, 0)),
                  pl.BlockSpec((d, tn), lambda j: (0, j)),
                  pl.BlockSpec((1, tn), lambda j: (0, j))],
        out_specs=pl.BlockSpec((r, tn), lambda j: (0, j)),
        out_shape=jax.ShapeDtypeStruct((r, n), F32),
        compiler_params=_params("arbitrary"),
        name="ada_mod",
    )(c_all, w, b.reshape(1, n))


def _ev_in_kernel(x_ref, sh_ref, sc_ref, w_ref, q_ref, z_ref, xbc_ref, kv_ref, dt_ref):
    x = x_ref[...]
    h = _rms(x) * (1.0 + sc_ref[...]) + sh_ref[...]
    y = jnp.dot(h.astype(BF16), w_ref[...], preferred_element_type=F32)
    q_ref[...] = y[:, 0:A_QW].astype(BF16)
    z_ref[...] = y[:, A_QW:A_QW + D_INNER]
    xbc_ref[...] = y[:, 1024:1024 + XBC_DIM]
    kv_ref[...] = y[:, 2048:2048 + 2 * A_KVW]
    dt_ref[...] = y[:, 2304:2304 + LANE]


def _ev_in(x, sh, sc, w):
    t, d = x.shape
    tm = _tile(t, 512)
    outs = [(A_QW, BF16), (D_INNER, F32), (XBC_DIM, F32), (2 * A_KVW, F32), (LANE, F32)]
    return pl.pallas_call(
        _ev_in_kernel,
        grid=(t // tm,),
        in_specs=[pl.BlockSpec((tm, d), lambda i: (i, 0)), _row_spec(sh, tm), _row_spec(sc, tm), _const_spec(w)],
        out_specs=[pl.BlockSpec((tm, n), lambda i: (i, 0)) for n, _ in outs],
        out_shape=[jax.ShapeDtypeStruct((t, n), dt) for n, dt in outs],
        compiler_params=_params("arbitrary"),
        name="ev_in_proj",
    )(x, sh, sc, w)


def _swa_heads(q, k_all, v_all, valid, distf, sink_ref):
    outs = []
    for g in range(A_KV_HEADS):
        k_g = k_all[:, A_HEAD_DIM * g:A_HEAD_DIM * (g + 1)].astype(BF16)
        v_g = v_all[:, A_HEAD_DIM * g:A_HEAD_DIM * (g + 1)].astype(BF16)
        for r in range(A_REP):
            h = g * A_REP + r
            q_h = q[:, A_HEAD_DIM * h:A_HEAD_DIM * (h + 1)]
            s = lax.dot_general(q_h, k_g, _NT, preferred_element_type=F32) * (A_HEAD_DIM ** -0.5)
            s = s - (2.0 ** (-8.0 * (h + 1) / A_HEADS)) * distf
            s = jnp.where(valid, s, -jnp.inf)
            sink = sink_ref[h]
            m = jnp.maximum(jnp.max(s, axis=-1, keepdims=True), sink)
            p = jnp.exp(s - m)
            den = jnp.sum(p, axis=-1, keepdims=True) + jnp.exp(sink - m)
            outs.append(jnp.dot(p.astype(BF16), v_g, preferred_element_type=F32) / den)
    return jnp.concatenate(outs, axis=-1)


def _swa_prompt_kernel(sink_ref, q_ref, kvp_ref, kvc_ref, o_ref):
    i = pl.program_id(0)
    kv = jnp.concatenate([kvp_ref[...], kvc_ref[...]], axis=0)
    row = lax.broadcasted_iota(jnp.int32, (WINDOW, 2 * WINDOW), 0)
    col = lax.broadcasted_iota(jnp.int32, (WINDOW, 2 * WINDOW), 1)
    dist = row + WINDOW - col
    first_key = jnp.where(i > 0, 0, WINDOW)
    valid = (dist >= 0) & (dist < WINDOW) & (col >= first_key)
    o = _swa_heads(q_ref[...], kv[:, :A_KVW], kv[:, A_KVW:], valid, dist.astype(F32), sink_ref)
    o_ref[...] = o.astype(BF16)


def _swa_prompt(q, kv, sinks):
    t = q.shape[0]
    nb = t // WINDOW
    return pl.pallas_call(
        _swa_prompt_kernel,
        grid=(nb,),
        in_specs=[pl.BlockSpec(memory_space=pltpu.SMEM),
                  pl.BlockSpec((WINDOW, A_QW), lambda i: (i, 0)),
                  pl.BlockSpec((WINDOW, 2 * A_KVW), lambda i: (jnp.maximum(i - 1, 0), 0)),
                  pl.BlockSpec((WINDOW, 2 * A_KVW), lambda i: (i, 0))],
        out_specs=pl.BlockSpec((WINDOW, A_QW), lambda i: (i, 0)),
        out_shape=jax.ShapeDtypeStruct((t, A_QW), BF16),
        compiler_params=_params("arbitrary"),
        name="swa_prompt",
    )(sinks, q, kv, kv)


def _swa_sample_kernel(sink_ref, q_ref, kvn_ref, buf_ref, o_ref, nbuf_ref, *, bb, l):
    w = WINDOW
    row = lax.broadcasted_iota(jnp.int32, (l, 2 * w), 0)
    col = lax.broadcasted_iota(jnp.int32, (l, 2 * w), 1)
    dist = row + w - col
    valid = (dist >= 0) & (dist < w)
    distf = dist.astype(F32)
    for b in range(bb):
        buf = buf_ref[b]
        kvn = kvn_ref[b]
        kv = jnp.concatenate([buf, kvn, jnp.zeros((w - l, 2 * A_KVW), F32)], axis=0)
        o = _swa_heads(q_ref[b], kv[:, :A_KVW], kv[:, A_KVW:], valid, distf, sink_ref)
        o_ref[b] = o.astype(BF16)
        nbuf_ref[b, 0:w - l, :] = buf[l:, :]
        nbuf_ref[b, w - l:w, :] = kvn


def _swa_sample(q, kvn, buf, sinks):
    b, l, _ = q.shape
    bb = _tile(b, 8)
    return pl.pallas_call(
        functools.partial(_swa_sample_kernel, bb=bb, l=l),
        grid=(b // bb,),
        in_specs=[pl.BlockSpec(memory_space=pltpu.SMEM),
                  pl.BlockSpec((bb, l, A_QW), lambda i: (i, 0, 0)),
                  pl.BlockSpec((bb, l, 2 * A_KVW), lambda i: (i, 0, 0)),
                  pl.BlockSpec((bb, WINDOW, 2 * A_KVW), lambda i: (i, 0, 0))],
        out_specs=[pl.BlockSpec((bb, l, A_QW), lambda i: (i, 0, 0)),
                   pl.BlockSpec((bb, WINDOW, 2 * A_KVW), lambda i: (i, 0, 0))],
        out_shape=[jax.ShapeDtypeStruct((b, l, A_QW), BF16),
                   jax.ShapeDtypeStruct((b, WINDOW, 2 * A_KVW), F32)],
        compiler_params=_params("arbitrary"),
        name="swa_sample",
    )(sinks, q, kvn, buf)


def _ssd_chunk(conv, z, dt, state, a_row, dsk, gn):
    L = SSD_CHUNK
    gw = D_INNER // B_GROUPS
    hpg = B_HEADS // B_GROUPS
    xs = conv[:, :D_INNER]
    adt = a_row * dt
    r2 = lax.broadcasted_iota(jnp.int32, (L, L), 0)
    c2 = lax.broadcasted_iota(jnp.int32, (L, L), 1)
    lower = r2 >= c2
    acs = jnp.dot(lower.astype(F32), adt, precision=_HI, preferred_element_type=F32)
    acs_t = acs.T
    band = lax.broadcasted_iota(jnp.int32, (L, gw), 1) // B_HEAD_DIM

    ys, states = [], []
    for g in range(B_GROUPS):
        b_g = conv[:, D_INNER + D_STATE * g:D_INNER + D_STATE * (g + 1)].astype(BF16)
        c_g = conv[:, D_INNER + B_GROUPS * D_STATE + D_STATE * g:
                   D_INNER + B_GROUPS * D_STATE + D_STATE * (g + 1)].astype(BF16)
        xs_g = xs[:, gw * g:gw * (g + 1)]

        def expand(mat, g=g):
            out = jnp.broadcast_to(mat[:, hpg * g:hpg * g + 1], (L, gw))
            for r in range(1, hpg):
                out = jnp.where(band == r, jnp.broadcast_to(mat[:, hpg * g + r:hpg * g + r + 1], (L, gw)), out)
            return out

        acs_e = expand(acs)
        xd = xs_g * expand(dt)
        xd_b = xd.astype(BF16)
        gmat = lax.dot_general(c_g, b_g, _NT, preferred_element_type=F32)
        y_diag = None
        for r in range(hpg):
            h = hpg * g + r
            diff = acs[:, h:h + 1] - acs_t[h:h + 1, :]
            lmat = jnp.exp(jnp.where(lower, diff, -jnp.inf))
            yr = jnp.dot((gmat * lmat).astype(BF16), xd_b, preferred_element_type=F32)
            y_diag = yr if y_diag is None else jnp.where(band == r, yr, y_diag)
        s_g = state[gw * g:gw * (g + 1), :]
        y_off = lax.dot_general(c_g, s_g.astype(BF16), _NT, preferred_element_type=F32) * jnp.exp(acs_e)
        decay = jnp.exp(acs_e[L - 1:L, :] - acs_e)
        upd = lax.dot_general((xd * decay).astype(BF16), b_g, _TN, preferred_element_type=F32)
        dec_rows = jnp.concatenate(
            [jnp.broadcast_to(jnp.exp(acs_t[hpg * g + r:hpg * g + r + 1, L - 1:L]), (B_HEAD_DIM, D_STATE))
             for r in range(hpg)], axis=0)
        states.append(s_g * dec_rows + upd)
        ys.append(y_diag + y_off + dsk[:, gw * g:gw * (g + 1)] * xs_g)

    outs = []
    for g in range(B_GROUPS):
        gt = ys[g] * _silu(z[:, gw * g:gw * (g + 1)])
        outs.append(_rms(gt))
    y = jnp.concatenate(outs, axis=-1) * gn
    return y, jnp.concatenate(states, axis=0)


def _conv_from_pad(xp_ref, cw_ref, cb_ref):
    L = SSD_CHUNK
    acc = cb_ref[...] + cw_ref[D_CONV - 1:D_CONV, :] * xp_ref[8:8 + L, :]
    for j in range(D_CONV - 1):
        acc = acc + cw_ref[j:j + 1, :] * xp_ref[5 + j:5 + j + L, :]
    return _silu(acc)


def _ssd_prompt_kernel(xbc_ref, z_ref, dt_ref, cw_ref, cb_ref, dtb_ref, a_ref, dsk_ref, gn_ref,
                       y_ref, st_ref, xp_ref, s_ref):
    i = pl.program_id(0)
    L = SSD_CHUNK

    @pl.when(i == 0)
    def _():
        xp_ref[0:8, :] = jnp.zeros((8, XBC_DIM), F32)
        s_ref[...] = jnp.zeros_like(s_ref)

    xp_ref[8:8 + L, :] = xbc_ref[...]
    conv = _conv_from_pad(xp_ref, cw_ref, cb_ref)
    xp_ref[0:8, :] = xp_ref[L:L + 8, :]
    dt = _softplus(dt_ref[...] + dtb_ref[...])
    y, new_state = _ssd_chunk(conv, z_ref[...], dt, s_ref[...], a_ref[...], dsk_ref[...], gn_ref[...])
    s_ref[...] = new_state
    y_ref[...] = y.astype(BF16)

    @pl.when(i == pl.num_programs(0) - 1)
    def _():
        st_ref[...] = new_state


def _ssd_prompt(xbc, z, dt, cw, cb, dtb, a_row, dsk, gn):
    t = xbc.shape[0]
    L = SSD_CHUNK
    consts = [cw, cb, dtb, a_row, dsk, gn]
    return pl.pallas_call(
        _ssd_prompt_kernel,
        grid=(t // L,),
        in_specs=[pl.BlockSpec((L, XBC_DIM), lambda i: (i, 0)),
                  pl.BlockSpec((L, D_INNER), lambda i: (i, 0)),
                  pl.BlockSpec((L, LANE), lambda i: (i, 0))] + [_const_spec(c) for c in consts],
        out_specs=[pl.BlockSpec((L, D_INNER), lambda i: (i, 0)),
                   pl.BlockSpec((B_HEADS * B_HEAD_DIM, D_STATE), lambda i: (0, 0))],
        out_shape=[jax.ShapeDtypeStruct((t, D_INNER), BF16),
                   jax.ShapeDtypeStruct((B_HEADS * B_HEAD_DIM, D_STATE), F32)],
        scratch_shapes=[pltpu.VMEM((L + 8, XBC_DIM), F32), pltpu.VMEM((B_HEADS * B_HEAD_DIM, D_STATE), F32)],
        compiler_params=_params("arbitrary"),
        name="ssd_prompt",
    )(xbc, z, dt, *consts)


def _ssd_sample_kernel(xbc_ref, z_ref, dt_ref, cs_ref, s0_ref, cw_ref, cb_ref, dtb_ref, a_ref, dsk_ref, gn_ref,
                       y_ref, st_ref, xp_ref, *, l):
    i = pl.program_id(0)
    L = SSD_CHUNK

    @pl.when(i == 0)
    def _():
        xp_ref[...] = jnp.zeros_like(xp_ref)

    xp_ref[5:8, :] = cs_ref[0]
    xp_ref[8:8 + l, :] = xbc_ref[0]
    conv = _conv_from_pad(xp_ref, cw_ref, cb_ref)
    pad = jnp.zeros((L - l, LANE), F32)
    dt = jnp.concatenate([_softplus(dt_ref[0] + dtb_ref[...]), pad], axis=0)
    z = jnp.concatenate([z_ref[0], jnp.zeros((L - l, D_INNER), F32)], axis=0)
    y, new_state = _ssd_chunk(conv, z, dt, s0_ref[0], a_ref[...], dsk_ref[...], gn_ref[...])
    y_ref[0] = y[0:l, :].astype(BF16)
    st_ref[0] = new_state


def _ssd_sample(xbc, z, dt, conv_state, ssm_state, cw, cb, dtb, a_row, dsk, gn):
    b, l, _ = xbc.shape
    L = SSD_CHUNK
    consts = [cw, cb, dtb, a_row, dsk, gn]
    hp = B_HEADS * B_HEAD_DIM
    return pl.pallas_call(
        functools.partial(_ssd_sample_kernel, l=l),
        grid=(b,),
        in_specs=[pl.BlockSpec((1, l, XBC_DIM), lambda i: (i, 0, 0)),
                  pl.BlockSpec((1, l, D_INNER), lambda i: (i, 0, 0)),
                  pl.BlockSpec((1, l, LANE), lambda i: (i, 0, 0)),
                  pl.BlockSpec((1, D_CONV - 1, XBC_DIM), lambda i: (i, 0, 0)),
                  pl.BlockSpec((1, hp, D_STATE), lambda i: (i, 0, 0))] + [_const_spec(c) for c in consts],
        out_specs=[pl.BlockSpec((1, l, D_INNER), lambda i: (i, 0, 0)),
                   pl.BlockSpec((1, hp, D_STATE), lambda i: (i, 0, 0))],
        out_shape=[jax.ShapeDtypeStruct((b, l, D_INNER), BF16),
                   jax.ShapeDtypeStruct((b, hp, D_STATE), F32)],
        scratch_shapes=[pltpu.VMEM((L + 8, XBC_DIM), F32)],
        compiler_params=_params("arbitrary"),
        name="ssd_sample",
    )(xbc, z, dt, conv_state, ssm_state, *consts)


def _proj_res_kernel(*refs, n_in):
    a_refs = refs[:n_in]
    x_ref, g_ref = refs[n_in:n_in + 2]
    w_refs = refs[n_in + 2:2 * n_in + 2]
    o_ref = refs[-1]
    acc = None
    for a_ref, w_ref in zip(a_refs, w_refs):
        part = jnp.dot(a_ref[...], w_ref[...], preferred_element_type=F32)
        acc = part if acc is None else acc + part
    o_ref[...] = x_ref[...] + g_ref[...] * acc


def _proj_res(acts, x, gate, ws):
    t, d = x.shape
    tm = _tile(t, 512)
    n_in = len(acts)
    return pl.pallas_call(
        functools.partial(_proj_res_kernel, n_in=n_in),
        grid=(t // tm,),
        in_specs=[pl.BlockSpec((tm, a.shape[1]), lambda i: (i, 0)) for a in acts]
        + [pl.BlockSpec((tm, d), lambda i: (i, 0)), _row_spec(gate, tm)] + [_const_spec(w) for w in ws],
        out_specs=pl.BlockSpec((tm, d), lambda i: (i, 0)),
        out_shape=jax.ShapeDtypeStruct((t, d), F32),
        compiler_params=_params("arbitrary"),
        name="proj_residual",
    )(*acts, x, gate, *ws)


def _swiglu_kernel(x_ref, sh_ref, sc_ref, g_ref, wgu_ref, wd_ref, o_ref, *, n_chunk):
    x = x_ref[...]
    h = (_rms(x) * (1.0 + sc_ref[...]) + sh_ref[...]).astype(BF16)
    f = wd_ref.shape[0]
    tf = f // n_chunk
    acc = None
    for c in range(n_chunk):
        gp = jnp.dot(h, wgu_ref[:, c * tf:(c + 1) * tf], preferred_element_type=F32)
        up = jnp.dot(h, wgu_ref[:, f + c * tf:f + (c + 1) * tf], preferred_element_type=F32)
        a = (_silu(gp) * up).astype(BF16)
        part = jnp.dot(a, wd_ref[c * tf:(c + 1) * tf, :], preferred_element_type=F32)
        acc = part if acc is None else acc + part
    o_ref[...] = x + g_ref[...] * acc


def _swiglu(x, sh, sc, gate, wgu, wd):
    t, d = x.shape
    tm = _tile(t, 512)
    return pl.pallas_call(
        functools.partial(_swiglu_kernel, n_chunk=2),
        grid=(t // tm,),
        in_specs=[pl.BlockSpec((tm, d), lambda i: (i, 0)), _row_spec(sh, tm), _row_spec(sc, tm), _row_spec(gate, tm),
                  _const_spec(wgu), _const_spec(wd)],
        out_specs=pl.BlockSpec((tm, d), lambda i: (i, 0)),
        out_shape=jax.ShapeDtypeStruct((t, d), F32),
        compiler_params=_params("arbitrary"),
        name="swiglu",
    )(x, sh, sc, gate, wgu, wd)


def _rope_lanes(blk, t_ref):
    half = C_ROPE // 2
    return (blk * t_ref[0] + pltpu.roll(blk, half, 1) * t_ref[1]
            + pltpu.roll(blk, LANE - half, 1) * t_ref[2])


def _od_latents(x_ref, sh_ref, sc_ref, w_in_ref, qn_ref, kvn_ref):
    x = x_ref[...]
    h = (_rms(x) * (1.0 + sc_ref[...]) + sh_ref[...]).astype(BF16)
    y = jnp.dot(h, w_in_ref[...], preferred_element_type=F32)
    cqn = (_rms(y[:, :Q_LORA]) * qn_ref[...]).astype(BF16)
    ckvn = _rms(y[:, Q_LORA:Q_LORA + KV_LORA]) * kvn_ref[...]
    return cqn, ckvn, y[:, Q_LORA + KV_LORA:]


def _od_in_prompt_kernel(x_ref, sh_ref, sc_ref, w_in_ref, qn_ref, kvn_ref, wuq_ref, wuk_ref, wuv_ref,
                         tq_ref, tk_ref, q_out, k_out, v_out, ckv_out, kpe_out):
    cqn, ckvn, kpe_pad = _od_latents(x_ref, sh_ref, sc_ref, w_in_ref, qn_ref, kvn_ref)
    ckv_out[...] = ckvn
    ckb = ckvn.astype(BF16)
    qf = jnp.dot(cqn, wuq_ref[...], preferred_element_type=F32)
    kf = jnp.dot(ckb, wuk_ref[...], preferred_element_type=F32)
    v_out[0] = lax.dot_general(wuv_ref[...], ckb, _NT, preferred_element_type=F32).astype(BF16)
    kpr = _rope_lanes(kpe_pad, tk_ref)
    kpe_out[...] = kpr[:, C_NOPE:C_NOPE + C_ROPE]
    for hh in range(C_HEADS):
        q_out[hh] = _rope_lanes(qf[:, LANE * hh:LANE * (hh + 1)], tq_ref).astype(BF16)
        k_out[hh] = (kf[:, LANE * hh:LANE * (hh + 1)] + kpr).astype(BF16)


def _od_in_prompt(x, sh, sc, w_in, qn, kvn, wuq, wuk, wuv, tq, tk):
    t, d = x.shape
    tm = _tile(t, MLA_TK)
    consts = [w_in, qn, kvn, wuq, wuk, wuv]
    return pl.pallas_call(
        _od_in_prompt_kernel,
        grid=(t // tm,),
        in_specs=[pl.BlockSpec((tm, d), lambda i: (i, 0)), _row_spec(sh, tm), _row_spec(sc, tm)]
        + [_const_spec(c) for c in consts]
        + [pl.BlockSpec((3, tm, LANE), lambda i: (0, i, 0)), pl.BlockSpec((3, tm, LANE), lambda i: (0, i, 0))],
        out_specs=[pl.BlockSpec((C_HEADS, tm, LANE), lambda i: (0, i, 0)),
                   pl.BlockSpec((C_HEADS, tm, LANE), lambda i: (0, i, 0)),
                   pl.BlockSpec((1, C_HEADS * C_V, tm), lambda i: (i, 0, 0)),
                   pl.BlockSpec((tm, KV_LORA), lambda i: (i, 0)),
                   pl.BlockSpec((tm, C_ROPE), lambda i: (i, 0))],
        out_shape=[jax.ShapeDtypeStruct((C_HEADS, t, LANE), BF16),
                   jax.ShapeDtypeStruct((C_HEADS, t, LANE), BF16),
                   jax.ShapeDtypeStruct((t // tm, C_HEADS * C_V, tm), BF16),
                   jax.ShapeDtypeStruct((t, KV_LORA), F32),
                   jax.ShapeDtypeStruct((t, C_ROPE), F32)],
        compiler_params=_params("arbitrary"),
        name="mla_proj_prompt",
    )(x, sh, sc, *consts, tq, tk)


def _od_in_sample_kernel(x_ref, sh_ref, sc_ref, w_in_ref, qn_ref, kvn_ref, wuq_ref, wukt_ref,
                         tq_ref, tk_ref, ql_out, qp_out, ckv_out, kpe_out, kpp_out):
    cqn, ckvn, kpe_pad = _od_latents(x_ref, sh_ref, sc_ref, w_in_ref, qn_ref, kvn_ref)
    ckv_out[...] = ckvn
    kpr = _rope_lanes(kpe_pad, tk_ref)
    kpe_out[...] = kpr[:, 0:C_ROPE]
    kpp_out[...] = kpr
    qf = jnp.dot(cqn, wuq_ref[...], preferred_element_type=F32)
    nq = C_HEADS * LANE
    for hh in range(C_HEADS):
        qn_h = (qf[:, LANE * hh:LANE * (hh + 1)] * MLA_SCALE).astype(BF16)
        ql = jnp.dot(qn_h, wukt_ref[hh], preferred_element_type=F32)
        ql_out[:, KV_LORA * hh:KV_LORA * (hh + 1)] = ql.astype(BF16)
        qp_out[:, LANE * hh:LANE * (hh + 1)] = _rope_lanes(qf[:, nq + LANE * hh:nq + LANE * (hh + 1)], tq_ref).astype(BF16)


def _od_in_sample(x, sh, sc, w_in, qn, kvn, wuq, wukt, tq, tk):
    t, d = x.shape
    tm = _tile(t, 512)
    consts = [w_in, qn, kvn, wuq, wukt]
    widths = [(C_HEADS * KV_LORA, BF16), (C_HEADS * LANE, BF16), (KV_LORA, F32), (C_ROPE, F32), (LANE, F32)]
    return pl.pallas_call(
        _od_in_sample_kernel,
        grid=(t // tm,),
        in_specs=[pl.BlockSpec((tm, d), lambda i: (i, 0)), _row_spec(sh, tm), _row_spec(sc, tm)]
        + [_const_spec(c) for c in consts]
        + [pl.BlockSpec((3, tm, LANE), lambda i: (0, i, 0)), pl.BlockSpec((3, tm, LANE), lambda i: (0, i, 0))],
        out_specs=[pl.BlockSpec((tm, n), lambda i: (i, 0)) for n, _ in widths],
        out_shape=[jax.ShapeDtypeStruct((t, n), dt) for n, dt in widths],
        compiler_params=_params("arbitrary"),
        name="mla_proj_sample",
    )(x, sh, sc, *consts, tq, tk)


def _flash_kernel(q_ref, k_ref, vt_ref, o_ref, m_ref, l_ref, acc_ref, sa_ref, mxa_ref, sb_ref, mxb_ref, *, tq, tk):
    qi = pl.program_id(1)
    m_ref[...] = jnp.full(m_ref.shape, -jnp.inf, F32)
    l_ref[...] = jnp.zeros(l_ref.shape, F32)
    acc_ref[...] = jnp.zeros(acc_ref.shape, F32)

    def scores(j, a, masked):
        start = pl.multiple_of(j * tk, tk)
        st = lax.dot_general(k_ref[a, pl.ds(start, tk), :], q_ref[a], _NT, preferred_element_type=F32)
        if masked:
            krow = lax.broadcasted_iota(jnp.int32, (tk, tq), 0) + (j * tk - qi * tq)
            qcol = lax.broadcasted_iota(jnp.int32, (tk, tq), 1)
            st = jnp.where(krow <= qcol, st, -jnp.inf)
        return st

    def absorb(j, a, st, mx):
        m_prev = m_ref[a]
        m_new = jnp.maximum(m_prev, mx)
        alpha = jnp.exp2(m_prev - m_new)
        p = jnp.exp2(st - m_new)
        l_ref[a] = alpha * l_ref[a] + jnp.sum(p, axis=0, keepdims=True)
        pv = jnp.dot(vt_ref[j, C_V * a:C_V * (a + 1), :], p.astype(BF16), preferred_element_type=F32)
        acc_ref[a] = alpha * acc_ref[a] + pv
        m_ref[a] = m_new

    n_full = (qi * tq) // tk
    for dd in range(max(tq // tk, 1)):
        for a in range(2):
            st = scores(n_full + dd, a, True)
            absorb(n_full + dd, a, st, jnp.max(st, axis=0, keepdims=True))

    def produce(j, buf):
        s_ref, mx_ref = buf
        for a in range(2):
            st = scores(j, a, False)
            s_ref[a] = st
            mx_ref[a] = jnp.max(st, axis=0, keepdims=True)

    def consume(j, buf):
        s_ref, mx_ref = buf
        for a in range(2):
            absorb(j, a, s_ref[a], mx_ref[a])

    buf_a, buf_b = (sa_ref, mxa_ref), (sb_ref, mxb_ref)

    @pl.when(n_full > 0)
    def _():
        produce(0, buf_a)
        n_loop = (n_full - 1) // 2

        def body(k, carry):
            produce(2 * k + 1, buf_b)
            consume(2 * k, buf_a)
            produce(2 * k + 2, buf_a)
            consume(2 * k + 1, buf_b)
            return carry

        lax.fori_loop(0, n_loop, body, 0)
        j0 = 2 * n_loop

        @pl.when(n_full - j0 == 2)
        def _():
            produce(j0 + 1, buf_b)
            consume(j0, buf_a)
            consume(j0 + 1, buf_b)

        @pl.when(n_full - j0 == 1)
        def _():
            consume(j0, buf_a)
    o_t = jnp.concatenate([acc_ref[0] / l_ref[0], acc_ref[1] / l_ref[1]], axis=0)
    o_ref[...] = o_t.T.astype(BF16)


def _flash_prompt(qh, kh, vt):
    nk, _, tk = vt.shape
    t = nk * tk
    tq = _tile(t, MLA_TQ)
    return pl.pallas_call(
        functools.partial(_flash_kernel, tq=tq, tk=tk),
        grid=(C_HEADS // 2, t // tq),
        in_specs=[pl.BlockSpec((2, tq, LANE), lambda hp, qi: (hp, qi, 0)),
                  pl.BlockSpec((2, t, LANE), lambda hp, qi: (hp, 0, 0)),
                  pl.BlockSpec((nk, 2 * C_V, tk), lambda hp, qi: (0, hp, 0))],
        out_specs=pl.BlockSpec((tq, LANE), lambda hp, qi: (qi, hp)),
        out_shape=jax.ShapeDtypeStruct((t, C_HEADS * C_V), BF16),
        scratch_shapes=[pltpu.VMEM((2, 1, tq), F32), pltpu.VMEM((2, 1, tq), F32), pltpu.VMEM((2, C_V, tq), F32),
                        pltpu.VMEM((2, tk, tq), F32), pltpu.VMEM((2, 1, tq), F32),
                        pltpu.VMEM((2, tk, tq), F32), pltpu.VMEM((2, 1, tq), F32)],
        compiler_params=_params("arbitrary", "arbitrary"),
        name="mla_flash_prompt",
    )(qh, kh, vt)


def _paged_kernel(pt_ref, ql_ref, qp_ref, cn_ref, kn_ref, ckv_hbm, kpt_hbm, o_ref,
                  ck_buf, kp_buf, ck_sem, kp_sem, ckb_ref, s_ref, mx_ref, m_ref, l_ref, acc_ref, *, ch, l):
    b = pl.program_id(0)
    nb = pl.num_programs(0)
    n_chunks = pt_ref.shape[1] // ch
    nq = ql_ref.shape[1]

    def ck_copy(bi, c, p, slot):
        page = pt_ref[bi, c * ch + p]
        return pltpu.make_async_copy(ckv_hbm.at[0, page], ck_buf.at[slot, pl.ds(p * PAGE_SIZE, PAGE_SIZE), :],
                                     ck_sem.at[slot])

    def kp_copy(bi, c, p, slot):
        page = pt_ref[bi, c * ch + p]
        return pltpu.make_async_copy(kpt_hbm.at[0, page], kp_buf.at[slot, 0:C_ROPE, pl.ds(p * PAGE_SIZE, PAGE_SIZE)],
                                     kp_sem.at[slot])

    def start_chunk(bi, c, slot):
        for p in range(ch):
            ck_copy(bi, c, p, slot).start()
            kp_copy(bi, c, p, slot).start()

    def wait_chunk(slot):
        for p in range(ch):
            ck_copy(0, 0, p, slot).wait()
            kp_copy(0, 0, p, slot).wait()

    @pl.when(b == 0)
    def _():
        kp_buf[...] = jnp.zeros_like(kp_buf)
        start_chunk(0, 0, 0)
        start_chunk(0, 1, 1)

    m_ref[...] = jnp.full(m_ref.shape, -jnp.inf, F32)
    l_ref[...] = jnp.zeros(l_ref.shape, F32)
    acc_ref[...] = jnp.zeros(acc_ref.shape, F32)
    ql = ql_ref[0]
    qp = qp_ref[0]

    def absorb(s, mx, values):
        m_prev = m_ref[...]
        m_new = jnp.maximum(m_prev, mx)
        alpha = jnp.exp(m_prev - m_new)
        p = jnp.exp(s - m_new)
        l_ref[...] = alpha * l_ref[...] + jnp.sum(p, axis=-1, keepdims=True)
        acc_ref[...] = alpha * acc_ref[...] + jnp.dot(p.astype(BF16), values, preferred_element_type=F32)
        m_ref[...] = m_new

    def produce(slot):
        ck = ck_buf[slot].astype(BF16)
        kp = kp_buf[slot].astype(BF16)
        s = lax.dot_general(ql, ck, _NT, preferred_element_type=F32) + jnp.dot(qp, kp, preferred_element_type=F32)
        ckb_ref[slot] = ck
        s_ref[slot] = s
        mx_ref[slot] = jnp.max(s, axis=-1, keepdims=True)

    def consume(slot):
        absorb(s_ref[slot], mx_ref[slot], ckb_ref[slot])

    wait_chunk(0)
    produce(0)

    def pair(k, carry):
        start_chunk(b, 2 * k + 2, 0)
        wait_chunk(1)
        produce(1)
        consume(0)
        start_chunk(b, 2 * k + 3, 1)
        wait_chunk(0)
        produce(0)
        consume(1)
        return carry

    lax.fori_loop(0, n_chunks // 2 - 1, pair, 0)

    @pl.when(b + 1 < nb)
    def _():
        start_chunk(b + 1, 0, 0)

    wait_chunk(1)
    produce(1)
    consume(0)

    @pl.when(b + 1 < nb)
    def _():
        start_chunk(b + 1, 1, 1)

    consume(1)

    cn = jnp.concatenate([cn_ref[0], jnp.zeros((LANE - l, KV_LORA), F32)], axis=0).astype(BF16)
    kn_t = jnp.concatenate([kn_ref[0], jnp.zeros((LANE - l, LANE), F32)], axis=0).T.astype(BF16)
    s = lax.dot_general(ql, cn, _NT, preferred_element_type=F32) + jnp.dot(qp, kn_t, preferred_element_type=F32)
    qtok = lax.broadcasted_iota(jnp.int32, (nq, LANE), 0) // C_HEADS
    kcol = lax.broadcasted_iota(jnp.int32, (nq, LANE), 1)
    s = jnp.where(kcol <= qtok, s, -jnp.inf)
    absorb(s, jnp.max(s, axis=-1, keepdims=True), cn)
    o_ref[0] = (acc_ref[...] / l_ref[...]).astype(BF16)


def _paged_attention(page_table, ql, qp, ckv_new, kpe_new_pad, cache_ckv, cache_kpe_t):
    b, nq, _ = ql.shape
    l = ckv_new.shape[1]
    n_pages = page_table.shape[1]
    ch = max(c for c in (16, 8, 4, 2, 1) if n_pages % (2 * c) == 0)
    grid_spec = pltpu.PrefetchScalarGridSpec(
        num_scalar_prefetch=1,
        grid=(b,),
        in_specs=[pl.BlockSpec((1, nq, KV_LORA), lambda bi, pt: (bi, 0, 0)),
                  pl.BlockSpec((1, nq, LANE), lambda bi, pt: (bi, 0, 0)),
                  pl.BlockSpec((1, l, KV_LORA), lambda bi, pt: (bi, 0, 0)),
                  pl.BlockSpec((1, l, LANE), lambda bi, pt: (bi, 0, 0)),
                  pl.BlockSpec(memory_space=pl.ANY),
                  pl.BlockSpec(memory_space=pl.ANY)],
        out_specs=pl.BlockSpec((1, nq, KV_LORA), lambda bi, pt: (bi, 0, 0)),
        scratch_shapes=[pltpu.VMEM((2, ch * PAGE_SIZE, KV_LORA), F32),
                        pltpu.VMEM((2, LANE, ch * PAGE_SIZE), F32),
                        pltpu.SemaphoreType.DMA((2,)),
                        pltpu.SemaphoreType.DMA((2,)),
                        pltpu.VMEM((2, ch * PAGE_SIZE, KV_LORA), BF16),
                        pltpu.VMEM((2, nq, ch * PAGE_SIZE), F32),
                        pltpu.VMEM((2, nq, 1), F32),
                        pltpu.VMEM((nq, 1), F32), pltpu.VMEM((nq, 1), F32), pltpu.VMEM((nq, KV_LORA), F32)],
    )
    return pl.pallas_call(
        functools.partial(_paged_kernel, ch=ch, l=l),
        grid_spec=grid_spec,
        out_shape=jax.ShapeDtypeStruct((b, nq, KV_LORA), BF16),
        compiler_params=_params("arbitrary"),
        name="mla_paged_sample",
    )(page_table, ql, qp, ckv_new, kpe_new_pad, cache_ckv, cache_kpe_t)


def _mla_out_sample_kernel(o_ref, x_ref, g_ref, wuv_ref, wo_ref, out_ref):
    parts = []
    for pr in range(C_HEADS // 2):
        parts.append(jnp.dot(o_ref[:, 2 * KV_LORA * pr:2 * KV_LORA * (pr + 1)], wuv_ref[pr],
                             preferred_element_type=F32))
    attn = jnp.concatenate(parts, axis=-1).astype(BF16)
    out_ref[...] = x_ref[...] + g_ref[...] * jnp.dot(attn, wo_ref[...], preferred_element_type=F32)


def _mla_out_sample(o_lat, x, gate, wuv_pairs, wo):
    t, d = x.shape
    tm = _tile(t, 512)
    return pl.pallas_call(
        _mla_out_sample_kernel,
        grid=(t // tm,),
        in_specs=[pl.BlockSpec((tm, o_lat.shape[1]), lambda i: (i, 0)), pl.BlockSpec((tm, d), lambda i: (i, 0)),
                  _row_spec(gate, tm), _const_spec(wuv_pairs), _const_spec(wo)],
        out_specs=pl.BlockSpec((tm, d), lambda i: (i, 0)),
        out_shape=jax.ShapeDtypeStruct((t, d), F32),
        compiler_params=_params("arbitrary"),
        name="mla_out_sample",
    )(o_lat, x, gate, wuv_pairs, wo)


def _route_kernel(x_ref, sh_ref, sc_ref, r_ref, h_ref, gate_ref, idx_ref):
    x = x_ref[...]
    h = _rms(x) * (1.0 + sc_ref[...]) + sh_ref[...]
    h_ref[...] = h
    logits = jnp.dot(h, r_ref[...], precision=_HI, preferred_element_type=F32)
    lane = lax.broadcasted_iota(jnp.int32, logits.shape, 1)
    logits = jnp.where(lane < N_EXPERTS, logits, -jnp.inf)
    v1 = jnp.max(logits, axis=-1, keepdims=True)
    i1 = jnp.min(jnp.where(logits == v1, lane, LANE), axis=-1, keepdims=True)
    rest = jnp.where(lane == i1, -jnp.inf, logits)
    v2 = jnp.max(rest, axis=-1, keepdims=True)
    i2 = jnp.min(jnp.where(rest == v2, lane, LANE), axis=-1, keepdims=True)
    e2 = jnp.exp(v2 - v1)
    gate_ref[...] = jnp.where(lane == 0, 1.0 / (1.0 + e2), jnp.where(lane == 1, e2 / (1.0 + e2), 0.0))
    idx_ref[...] = jnp.where(lane == 0, i1, jnp.where(lane == 1, i2, 0))


def _route(x, sh, sc, router_pad):
    t, d = x.shape
    tm = _tile(t, 512)
    return pl.pallas_call(
        _route_kernel,
        grid=(t // tm,),
        in_specs=[pl.BlockSpec((tm, d), lambda i: (i, 0)), _row_spec(sh, tm), _row_spec(sc, tm), _const_spec(router_pad)],
        out_specs=[pl.BlockSpec((tm, d), lambda i: (i, 0)), pl.BlockSpec((tm, LANE), lambda i: (i, 0)),
                   pl.BlockSpec((tm, LANE), lambda i: (i, 0))],
        out_shape=[jax.ShapeDtypeStruct((t, d), F32), jax.ShapeDtypeStruct((t, LANE), F32),
                   jax.ShapeDtypeStruct((t, LANE), jnp.int32)],
        compiler_params=_params("arbitrary"),
        name="moe_route",
    )(x, sh, sc, router_pad)


def _moe_plan(idx, tm):
    t = idx.shape[0]
    n_tiles = -(-2 * t // tm) + N_EXPERTS
    e_flat = idx.reshape(-1)
    onehot = (e_flat[:, None] == jnp.arange(N_EXPERTS, dtype=jnp.int32)[None]).astype(jnp.int32)
    csum = jnp.cumsum(onehot, axis=0)
    rank = jnp.sum(csum * onehot, axis=1) - 1
    padded = (csum[-1] + tm - 1) // tm * tm
    gend = jnp.cumsum(padded)
    dest = (gend - padded)[e_flat] + rank
    row_token = jnp.zeros((n_tiles * tm,), jnp.int32).at[dest].set(jnp.arange(2 * t, dtype=jnp.int32) // 2)
    n_used = gend[-1] // tm
    tile_start = jnp.minimum(jnp.arange(n_tiles, dtype=jnp.int32), n_used - 1) * tm
    tile_expert = jnp.sum((gend[None, :] <= tile_start[:, None]).astype(jnp.int32), axis=1)
    return dest, row_token, tile_expert.astype(jnp.int32), n_used.reshape(1).astype(jnp.int32)


def _row_loop(n, fn):
    def body(r, carry):
        fn(r)
        return carry

    lax.fori_loop(0, n, body, 0, unroll=8)


def _moe_gather_kernel(rt_ref, h_hbm, xs_ref, sem, *, tm):
    i = pl.program_id(0)

    def row_copy(r, tok):
        return pltpu.make_async_copy(h_hbm.at[pl.ds(tok, 1)], xs_ref.at[pl.ds(r, 1)], sem.at[0])

    _row_loop(tm, lambda r: row_copy(r, rt_ref[i * tm + r]).start())
    _row_loop(tm, lambda r: row_copy(0, 0).wait())


def _moe_gather(row_token, h, tm):
    r = row_token.shape[0]
    d = h.shape[1]
    return pl.pallas_call(
        functools.partial(_moe_gather_kernel, tm=tm),
        grid_spec=pltpu.PrefetchScalarGridSpec(
            num_scalar_prefetch=1, grid=(r // tm,),
            in_specs=[pl.BlockSpec(memory_space=pl.ANY)],
            out_specs=pl.BlockSpec((tm, d), lambda i, rt: (i, 0)),
            scratch_shapes=[pltpu.SemaphoreType.DMA((1,))]),
        out_shape=jax.ShapeDtypeStruct((r, d), h.dtype),
        compiler_params=_params("arbitrary"),
        name="moe_gather",
    )(row_token, h)


def _moe_grouped_kernel(te_ref, nu_ref, xs_ref, wg_ref, wu_ref, wd_ref, y_ref, xb_ref):
    del te_ref
    i = pl.program_id(0)
    c = pl.program_id(1)

    @pl.when(c == 0)
    def _():
        xb_ref[...] = xs_ref[...].astype(BF16)
        y_ref[...] = jnp.zeros_like(y_ref)

    @pl.when(i < nu_ref[0])
    def _():
        xb = xb_ref[...]
        gp = jnp.dot(xb, wg_ref[0], preferred_element_type=F32)
        up = jnp.dot(xb, wu_ref[0], preferred_element_type=F32)
        y_ref[...] += jnp.dot((_silu(gp) * up).astype(BF16), wd_ref[0], preferred_element_type=F32)


def _moe_grouped(tile_expert, n_used, xs, wgu, wd, tm):
    r, d = xs.shape
    f = wd.shape[1]
    tf = 512
    nc = f // tf

    def chunk(i, c, nu):
        return jnp.where(i < nu[0], c, nc - 1)

    return pl.pallas_call(
        _moe_grouped_kernel,
        grid_spec=pltpu.PrefetchScalarGridSpec(
            num_scalar_prefetch=2, grid=(r // tm, nc),
            in_specs=[pl.BlockSpec((tm, d), lambda i, c, te, nu: (i, 0)),
                      pl.BlockSpec((1, d, tf), lambda i, c, te, nu: (te[i], 0, chunk(i, c, nu))),
                      pl.BlockSpec((1, d, tf), lambda i, c, te, nu: (te[i], 0, nc + chunk(i, c, nu))),
                      pl.BlockSpec((1, tf, d), lambda i, c, te, nu: (te[i], chunk(i, c, nu), 0))],
            out_specs=pl.BlockSpec((tm, d), lambda i, c, te, nu: (i, 0)),
            scratch_shapes=[pltpu.VMEM((tm, d), BF16)]),
        out_shape=jax.ShapeDtypeStruct((r, d), F32),
        compiler_params=_params("arbitrary", "arbitrary"),
        name="moe_grouped",
    )(tile_expert, n_used, xs, wgu, wgu, wd)


def _moe_combine_kernel(pos_ref, x_ref, g2_ref, gate_ref, fn_ref, y_hbm, o_ref, ya_ref, yb_ref, sem, *, tm):
    i = pl.program_id(0)
    slot = i % 2

    def row_copy(buf, s, r, src):
        return pltpu.make_async_copy(y_hbm.at[pl.ds(src, 1)], buf.at[s, pl.ds(r, 1)], sem.at[s])

    def issue(step, s):
        def one(r):
            t = step * tm + r
            row_copy(ya_ref, s, r, pos_ref[2 * t]).start()
            row_copy(yb_ref, s, r, pos_ref[2 * t + 1]).start()

        _row_loop(tm, one)

    @pl.when(i == 0)
    def _():
        issue(0, 0)

    @pl.when(i + 1 < pl.num_programs(0))
    def _():
        issue(i + 1, 1 - slot)

    def wait_one(r):
        row_copy(ya_ref, slot, 0, 0).wait()
        row_copy(yb_ref, slot, 0, 0).wait()

    _row_loop(tm, wait_one)
    gate = gate_ref[...]
    y = gate[:, 0:1] * ya_ref[slot] + gate[:, 1:2] * yb_ref[slot]
    xo = x_ref[...] + g2_ref[...] * y
    o_ref[...] = _rms(xo) * fn_ref[...]


def _moe_combine(pos, x, g2, gate, fn, y):
    t, d = x.shape
    tm = _tile(t, 256)
    return pl.pallas_call(
        functools.partial(_moe_combine_kernel, tm=tm),
        grid_spec=pltpu.PrefetchScalarGridSpec(
            num_scalar_prefetch=1, grid=(t // tm,),
            in_specs=[pl.BlockSpec((tm, d), lambda i, p: (i, 0)),
                      (pl.BlockSpec((1, d), lambda i, p: (0, 0)) if g2.shape[0] == 1
                       else pl.BlockSpec((tm, d), lambda i, p: (i, 0))),
                      pl.BlockSpec((tm, LANE), lambda i, p: (i, 0)),
                      pl.BlockSpec((1, d), lambda i, p: (0, 0)),
                      pl.BlockSpec(memory_space=pl.ANY)],
            out_specs=pl.BlockSpec((tm, d), lambda i, p: (i, 0)),
            scratch_shapes=[pltpu.VMEM((2, tm, d), F32), pltpu.VMEM((2, tm, d), F32), pltpu.SemaphoreType.DMA((2,))]),
        out_shape=jax.ShapeDtypeStruct((t, d), F32),
        compiler_params=_params("arbitrary"),
        name="moe_combine",
    )(pos, x, g2, gate, fn, y)


def _rope_tables(pos, lo, reps, scale):
    half = C_ROPE // 2
    freqs = ROPE_THETA ** (-2.0 * jnp.arange(half, dtype=F32) / C_ROPE)
    ang = pos.astype(F32)[:, None] * freqs[None]
    cos, sin = jnp.cos(ang), jnp.sin(ang)
    n = pos.shape[0]
    zeros = jnp.zeros((n, half), F32)
    c_grp = jnp.concatenate([cos, cos], axis=1)
    s1_grp = jnp.concatenate([zeros, sin], axis=1)
    s2_grp = jnp.concatenate([-sin, zeros], axis=1)

    def lay(grp, fill):
        body = jnp.tile(grp, (1, reps))
        left = jnp.full((n, lo), fill, F32)
        right = jnp.zeros((n, LANE - lo - reps * C_ROPE), F32)
        return jnp.concatenate([left, body, right], axis=1)

    return jnp.stack([lay(c_grp, 1.0), lay(s1_grp, 0.0), lay(s2_grp, 0.0)]) * scale


def _pad_cols(w, n):
    return jnp.pad(w, ((0, 0), (0, n - w.shape[1])))


def kernel(x_prompt, x_sample, c_prompt, c_sample, state_swa_kv, state_conv, state_ssm, cache_ckv, cache_kpe, page_table,
           ev_mod_w, ev_mod_b, ev_w_in, ev_sinks, ev_conv_w, ev_conv_b, ev_dt_bias, ev_a_log, ev_d_skip, ev_gnorm,
           ev_w_out, ev_w_gu, ev_w_down, od_mod_w, od_mod_b, od_w_in, od_qnorm, od_kvnorm, od_w_uq, od_w_uk, od_w_uv,
           od_w_out, od_router, od_w_gu, od_w_down, final_norm):
    d = D_MODEL
    _, tp, _ = x_prompt.shape
    bs, ls, _ = x_sample.shape
    ts = bs * ls
    n_pages = page_table.shape[1]
    past = n_pages * PAGE_SIZE
    assert state_swa_kv.shape[2] == WINDOW and ls <= 8 and ls >= D_CONV - 1

    xp = x_prompt.reshape(tp, d)
    xs = x_sample.reshape(ts, d)

    n_c = 1 + bs
    n_cp = -(-n_c // 8) * 8
    c_all = jnp.pad(jnp.concatenate([c_prompt, c_sample], axis=0), ((0, n_cp - n_c), (0, 0)))

    def mods(w, b):
        m = _ada_mod(c_all, w, b)
        mp = [m[0:1, k * d:(k + 1) * d] for k in range(MOD_SLOTS)]
        ms = [jnp.repeat(m[1:n_c, k * d:(k + 1) * d], ls, axis=0) for k in range(MOD_SLOTS)]
        return mp, ms

    i = 0
    mp, ms = mods(ev_mod_w[i], ev_mod_b[i])
    w_in = ev_w_in[i]
    o_k, o_v, o_z, o_x, o_dt = A_QW, A_QW + A_KVW, A_QW + 2 * A_KVW, A_QW + 2 * A_KVW + D_INNER, A_QW + 2 * A_KVW + D_INNER + XBC_DIM
    w_in_p = jnp.concatenate([w_in[:, :o_k], w_in[:, o_z:o_x], w_in[:, o_x:o_dt], w_in[:, o_k:o_z],
                              _pad_cols(w_in[:, o_dt:], LANE)], axis=1).astype(BF16)
    sinks = ev_sinks[i].astype(F32)
    cw = ev_conv_w[i]
    cb = ev_conv_b[i].reshape(1, XBC_DIM)
    dtb = _pad_cols(ev_dt_bias[i].reshape(1, B_HEADS).astype(F32), LANE)
    a_row = _pad_cols(-jnp.exp(ev_a_log[i].astype(F32)).reshape(1, B_HEADS), LANE)
    dsk = jnp.repeat(ev_d_skip[i].astype(F32), B_HEAD_DIM).reshape(1, D_INNER)
    gn = ev_gnorm[i].reshape(1, D_INNER)
    w_out = ev_w_out[i].astype(BF16)
    w_gu = ev_w_gu[i].astype(BF16)
    w_dn = ev_w_down[i].astype(BF16)
    ssd_consts = (cw, cb, dtb, a_row, dsk, gn)

    q, z, xbc, kv, dt = _ev_in(xp, mp[0], mp[1], w_in_p)
    attn = _swa_prompt(q, kv, sinks)
    ssm, ssm_state_p = _ssd_prompt(xbc, z, dt, *ssd_consts)
    xp = _proj_res([attn, ssm], xp, mp[2], [w_out[:A_QW], w_out[A_QW:]])
    xp = _swiglu(xp, mp[3], mp[4], mp[5], w_gu, w_dn)
    swa_kv_prompt = kv[tp - WINDOW:].reshape(1, 1, WINDOW, 2, A_KV_HEADS, A_HEAD_DIM)
    conv_prompt = xbc[tp - (D_CONV - 1):].reshape(1, 1, D_CONV - 1, XBC_DIM)
    ssm_prompt = ssm_state_p.reshape(1, 1, B_HEADS, B_HEAD_DIM, D_STATE)

    q, z, xbc, kv, dt = _ev_in(xs, ms[0], ms[1], w_in_p)
    buf = state_swa_kv[i].reshape(bs, WINDOW, 2 * A_KVW)
    attn, nbuf = _swa_sample(q.reshape(bs, ls, A_QW), kv.reshape(bs, ls, 2 * A_KVW), buf, sinks)
    xbc3 = xbc.reshape(bs, ls, XBC_DIM)
    ssm, ssm_state_s = _ssd_sample(xbc3, z.reshape(bs, ls, D_INNER), dt.reshape(bs, ls, LANE), state_conv[i],
                                   state_ssm[i].reshape(bs, B_HEADS * B_HEAD_DIM, D_STATE), *ssd_consts)
    xs = _proj_res([attn.reshape(ts, A_QW), ssm.reshape(ts, D_INNER)], xs, ms[2], [w_out[:A_QW], w_out[A_QW:]])
    xs = _swiglu(xs, ms[3], ms[4], ms[5], w_gu, w_dn)
    swa_kv_sample = nbuf.reshape(1, bs, WINDOW, 2, A_KV_HEADS, A_HEAD_DIM)
    conv_sample = xbc3[:, ls - (D_CONV - 1):].reshape(1, bs, D_CONV - 1, XBC_DIM)
    ssm_sample = ssm_state_s.reshape(1, bs, B_HEADS, B_HEAD_DIM, D_STATE)

    mp, ms = mods(od_mod_w[i], od_mod_b[i])
    w_in = od_w_in[i]
    w_cq, w_ckv, w_kpe = w_in[:, :Q_LORA], w_in[:, Q_LORA:Q_LORA + KV_LORA], w_in[:, Q_LORA + KV_LORA:]
    qn = od_qnorm[i].reshape(1, Q_LORA)
    kvn = od_kvnorm[i].reshape(1, KV_LORA)
    w_uq = od_w_uq[i].reshape(Q_LORA, C_HEADS, C_NOPE + C_ROPE)
    w_uk = od_w_uk[i]
    w_uv = od_w_uv[i]
    w_o = od_w_out[i].astype(BF16)

    w_in_pp = jnp.concatenate([w_cq, w_ckv, jnp.pad(w_kpe, ((0, 0), (C_NOPE, LANE - C_NOPE - C_ROPE)))], axis=1).astype(BF16)
    wuq_p = jnp.pad(w_uq, ((0, 0), (0, 0), (0, LANE - C_NOPE - C_ROPE))).reshape(Q_LORA, C_HEADS * LANE).astype(BF16)
    wuk_p = jnp.pad(w_uk, ((0, 0), (0, 0), (0, LANE - C_NOPE))).reshape(KV_LORA, C_HEADS * LANE).astype(BF16)
    wuv_p = w_uv.reshape(KV_LORA, C_HEADS * C_V).T.astype(BF16)
    pos_p = jnp.arange(tp, dtype=jnp.int32)
    tq_p = _rope_tables(pos_p, C_NOPE, 1, MLA_SCALE * LOG2E)
    tk_p = _rope_tables(pos_p, C_NOPE, 1, 1.0)
    qh, kh, v, ckv_p, kpe_p = _od_in_prompt(xp, mp[0], mp[1], w_in_pp, qn, kvn, wuq_p, wuk_p, wuv_p, tq_p, tk_p)
    attn = _flash_prompt(qh, kh, v)
    xp = _proj_res([attn], xp, mp[2], [w_o])

    w_in_ps = jnp.concatenate([w_cq, w_ckv, _pad_cols(w_kpe, LANE)], axis=1).astype(BF16)
    wq_nope = jnp.pad(w_uq[:, :, :C_NOPE], ((0, 0), (0, 0), (0, LANE - C_NOPE))).reshape(Q_LORA, C_HEADS * LANE)
    wq_rope = jnp.pad(w_uq[:, :, C_NOPE:], ((0, 0), (0, 0), (0, LANE - C_ROPE))).reshape(Q_LORA, C_HEADS * LANE)
    wuq_s = jnp.concatenate([wq_nope, wq_rope], axis=1).astype(BF16)
    wukt_s = jnp.pad(jnp.transpose(w_uk, (1, 2, 0)), ((0, 0), (0, LANE - C_NOPE), (0, 0))).astype(BF16)
    pos_s = jnp.tile(past + jnp.arange(ls, dtype=jnp.int32), bs)
    tq_s = _rope_tables(pos_s, 0, 1, MLA_SCALE)
    tk_s = _rope_tables(pos_s, 0, 1, 1.0)
    ql, qp, ckv_s, kpe_s, kpe_pad_s = _od_in_sample(xs, ms[0], ms[1], w_in_ps, qn, kvn, wuq_s, wukt_s, tq_s, tk_s)
    nq = ls * C_HEADS
    o_lat = _paged_attention(page_table, ql.reshape(bs, nq, KV_LORA), qp.reshape(bs, nq, LANE),
                             ckv_s.reshape(bs, ls, KV_LORA), kpe_pad_s.reshape(bs, ls, LANE),
                             cache_ckv[i:i + 1], jnp.swapaxes(cache_kpe[i:i + 1], 2, 3))
    wuv_h = jnp.transpose(w_uv, (1, 0, 2)).reshape(C_HEADS // 2, 2, KV_LORA, C_V)
    zero = jnp.zeros((C_HEADS // 2, KV_LORA, C_V), F32)
    wuv_pairs = jnp.concatenate([jnp.concatenate([wuv_h[:, 0], zero], axis=2),
                                 jnp.concatenate([zero, wuv_h[:, 1]], axis=2)], axis=1).astype(BF16)
    xs = _mla_out_sample(o_lat.reshape(ts, C_HEADS * KV_LORA), xs, ms[2], wuv_pairs, w_o)

    router_pad = _pad_cols(od_router[i].astype(F32), LANE)
    wgu_e = od_w_gu[i].astype(BF16)
    wdn_e = od_w_down[i].astype(BF16)
    fn = final_norm.reshape(1, d).astype(F32)
    h_p, gate_p, idx_p = _route(xp, mp[3], mp[4], router_pad)
    h_s, gate_s, idx_s = _route(xs, ms[3], ms[4], router_pad)
    idx_all = jnp.concatenate([idx_p[:, :2], idx_s[:, :2]], axis=0)
    dest, row_token, tile_expert, n_used = _moe_plan(idx_all, MOE_TM)
    xs_sorted = _moe_gather(row_token, jnp.concatenate([h_p, h_s], axis=0), MOE_TM)
    y_sorted = _moe_grouped(tile_expert, n_used, xs_sorted, wgu_e, wdn_e, MOE_TM)
    y_prompt = _moe_combine(dest[:2 * tp], xp, mp[5], gate_p, fn, y_sorted)
    y_sample = _moe_combine(dest[2 * tp:], xs, ms[5], gate_s, fn, y_sorted)

    return (y_prompt.reshape(1, tp, d), y_sample.reshape(bs, ls, d),
            swa_kv_prompt, swa_kv_sample, conv_prompt, conv_sample, ssm_prompt, ssm_sample,
            ckv_p.reshape(1, 1, tp, KV_LORA), ckv_s.reshape(1, bs, ls, KV_LORA),
            kpe_p.reshape(1, 1, tp, C_ROPE), kpe_s.reshape(1, bs, ls, C_ROPE))
```

```python
import functools
import math

import jax
import jax.numpy as jnp
from jax import lax
from jax.experimental import pallas as pl
from jax.experimental.pallas import tpu as pltpu

F32 = jnp.float32
BF16 = jnp.bfloat16

D_MODEL = 1024
EPS = 1e-6
MOD_SLOTS = 6

A_HEAD_DIM = 64
A_HEADS = 8
A_KV_HEADS = 2
A_REP = A_HEADS // A_KV_HEADS
WINDOW = 128
A_QW = A_HEADS * A_HEAD_DIM
A_KVW = A_KV_HEADS * A_HEAD_DIM

D_INNER = 512
B_HEAD_DIM = 64
B_HEADS = 8
B_GROUPS = 2
D_STATE = 128
D_CONV = 4
SSD_CHUNK = 128
XBC_DIM = D_INNER + 2 * B_GROUPS * D_STATE

C_HEADS = 16
C_NOPE = 64
C_ROPE = 32
C_V = 64
Q_LORA = 384
KV_LORA = 256
ROPE_THETA = 10000.0
MLA_SCALE = (C_NOPE + C_ROPE) ** -0.5
PAGE_SIZE = 128

D_FF = 2816
N_EXPERTS = 8
D_FF_EXPERT = 3584

LANE = 128
MLA_TK = 512
MLA_TQ = 512
LOG2E = 1.4426950408889634
MOE_TM = 512
VMEM_LIMIT = 56 * 1024 * 1024

_HI = lax.Precision.HIGHEST
_NT = (((1,), (1,)), ((), ()))
_TN = (((0,), (0,)), ((), ()))


def _params(*sem):
    return pltpu.CompilerParams(dimension_semantics=sem, vmem_limit_bytes=VMEM_LIMIT)


def _tile(n, pref):
    t = min(n, pref)
    while n % t:
        t -= 8
    return t


def _rms(x):
    return x * lax.rsqrt(jnp.mean(x * x, axis=-1, keepdims=True) + EPS)


def _silu(x):
    return x * jax.nn.sigmoid(x)


def _softplus(x):
    e = jnp.exp(-jnp.abs(x))
    u = 1.0 + e
    lg = jnp.where(u == 1.0, e, jnp.log(u) * e / (u - 1.0))
    return jnp.maximum(x, 0.0) + lg


def _row_spec(arr, tm):
    d = arr.shape[1]
    if arr.shape[0] == 1:
        return pl.BlockSpec((1, d), lambda i: (0, 0))
    return pl.BlockSpec((tm, d), lambda i: (i, 0))


def _const_spec(arr):
    nd = arr.ndim
    return pl.BlockSpec(arr.shape, lambda *_: (0,) * nd)


def _mod_kernel(c_ref, w_ref, b_ref, o_ref):
    s = _silu(c_ref[...]).astype(BF16)
    o_ref[...] = jnp.dot(s, w_ref[...].astype(BF16), preferred_element_type=F32) + b_ref[...]


def _ada_mod(c_all, w, b):
    r, d = c_all.shape
    n = w.shape[1]
    tn = 1024
    return pl.pallas_call(
        _mod_kernel,
        grid=(n // tn,),
        in_specs=[pl.BlockSpec((r, d), lambda j: (0, 0)),
                  pl.BlockSpec((d, tn), lambda j: (0, j)),
                  pl.BlockSpec((1, tn), lambda j: (0, j))],
        out_specs=pl.BlockSpec((r, tn), lambda j: (0, j)),
        out_shape=jax.ShapeDtypeStruct((r, n), F32),
        compiler_params=_params("arbitrary"),
        name="ada_mod",
    )(c_all, w, b.reshape(1, n))


def _ev_in_kernel(x_ref, sh_ref, sc_ref, w_ref, q_ref, z_ref, xbc_ref, kv_ref, dt_ref):
    x = x_ref[...]
    h = _rms(x) * (1.0 + sc_ref[...]) + sh_ref[...]
    y = jnp.dot(h.astype(BF16), w_ref[...], preferred_element_type=F32)
    q_ref[...] = y[:, 0:A_QW].astype(BF16)
    z_ref[...] = y[:, A_QW:A_QW + D_INNER]
    xbc_ref[...] = y[:, 1024:1024 + XBC_DIM]
    kv_ref[...] = y[:, 2048:2048 + 2 * A_KVW]
    dt_ref[...] = y[:, 2304:2304 + LANE]


def _ev_in(x, sh, sc, w):
    t, d = x.shape
    tm = _tile(t, 512)
    outs = [(A_QW, BF16), (D_INNER, F32), (XBC_DIM, F32), (2 * A_KVW, F32), (LANE, F32)]
    return pl.pallas_call(
        _ev_in_kernel,
        grid=(t // tm,),
        in_specs=[pl.BlockSpec((tm, d), lambda i: (i, 0)), _row_spec(sh, tm), _row_spec(sc, tm), _const_spec(w)],
        out_specs=[pl.BlockSpec((tm, n), lambda i: (i, 0)) for n, _ in outs],
        out_shape=[jax.ShapeDtypeStruct((t, n), dt) for n, dt in outs],
        compiler_params=_params("arbitrary"),
        name="ev_in_proj",
    )(x, sh, sc, w)


def _swa_heads(q, k_all, v_all, valid, distf, sink_ref):
    lq = q.shape[0]
    rep = lax.broadcasted_iota(jnp.int32, (A_REP * lq, 1), 0) // lq
    valid_s = jnp.concatenate([valid] * A_REP, axis=0)
    dist_s = jnp.concatenate([distf] * A_REP, axis=0)

    def per_head(g, vals):
        col = jnp.full((A_REP * lq, 1), vals[g * A_REP], F32)
        for r in range(1, A_REP):
            col = jnp.where(rep == r, vals[g * A_REP + r], col)
        return col

    slopes = [2.0 ** (-8.0 * (h + 1) / A_HEADS) for h in range(A_HEADS)]
    sinks = [sink_ref[h] for h in range(A_HEADS)]
    outs = []
    for g in range(A_KV_HEADS):
        k_g = k_all[:, A_HEAD_DIM * g:A_HEAD_DIM * (g + 1)].astype(BF16)
        v_g = v_all[:, A_HEAD_DIM * g:A_HEAD_DIM * (g + 1)].astype(BF16)
        q_g = jnp.concatenate([q[:, A_HEAD_DIM * h:A_HEAD_DIM * (h + 1)]
                               for h in range(g * A_REP, (g + 1) * A_REP)], axis=0)
        s = lax.dot_general(q_g, k_g, _NT, preferred_element_type=F32) * (A_HEAD_DIM ** -0.5)
        s = s - per_head(g, slopes) * dist_s
        s = jnp.where(valid_s, s, -jnp.inf)
        sink = per_head(g, sinks)
        m = jnp.maximum(jnp.max(s, axis=-1, keepdims=True), sink)
        p = jnp.exp(s - m)
        den = jnp.sum(p, axis=-1, keepdims=True) + jnp.exp(sink - m)
        o = jnp.dot(p.astype(BF16), v_g, preferred_element_type=F32) / den
        outs.extend(o[r * lq:(r + 1) * lq] for r in range(A_REP))
    return jnp.concatenate(outs, axis=-1)


def _swa_prompt_kernel(sink_ref, q_ref, kvp_ref, kvc_ref, o_ref):
    i = pl.program_id(0)
    kv = jnp.concatenate([kvp_ref[...], kvc_ref[...]], axis=0)
    row = lax.broadcasted_iota(jnp.int32, (WINDOW, 2 * WINDOW), 0)
    col = lax.broadcasted_iota(jnp.int32, (WINDOW, 2 * WINDOW), 1)
    dist = row + WINDOW - col
    first_key = jnp.where(i > 0, 0, WINDOW)
    valid = (dist >= 0) & (dist < WINDOW) & (col >= first_key)
    o = _swa_heads(q_ref[...], kv[:, :A_KVW], kv[:, A_KVW:], valid, dist.astype(F32), sink_ref)
    o_ref[...] = o.astype(BF16)


def _swa_prompt(q, kv, sinks):
    t = q.shape[0]
    nb = t // WINDOW
    return pl.pallas_call(
        _swa_prompt_kernel,
        grid=(nb,),
        in_specs=[pl.BlockSpec(memory_space=pltpu.SMEM),
                  pl.BlockSpec((WINDOW, A_QW), lambda i: (i, 0)),
                  pl.BlockSpec((WINDOW, 2 * A_KVW), lambda i: (jnp.maximum(i - 1, 0), 0)),
                  pl.BlockSpec((WINDOW, 2 * A_KVW), lambda i: (i, 0))],
        out_specs=pl.BlockSpec((WINDOW, A_QW), lambda i: (i, 0)),
        out_shape=jax.ShapeDtypeStruct((t, A_QW), BF16),
        compiler_params=_params("arbitrary"),
        name="swa_prompt",
    )(sinks, q, kv, kv)


def _swa_sample_kernel(sink_ref, q_ref, kvn_ref, buf_ref, o_ref, nbuf_ref, *, bb, l):
    w = WINDOW
    row = lax.broadcasted_iota(jnp.int32, (l, 2 * w), 0)
    col = lax.broadcasted_iota(jnp.int32, (l, 2 * w), 1)
    dist = row + w - col
    valid = (dist >= 0) & (dist < w)
    distf = dist.astype(F32)
    for b in range(bb):
        buf = buf_ref[b]
        kvn = kvn_ref[b]
        kv = jnp.concatenate([buf, kvn, jnp.zeros((w - l, 2 * A_KVW), F32)], axis=0)
        o = _swa_heads(q_ref[b], kv[:, :A_KVW], kv[:, A_KVW:], valid, distf, sink_ref)
        o_ref[b] = o.astype(BF16)
        nbuf_ref[b, 0:w - l, :] = buf[l:, :]
        nbuf_ref[b, w - l:w, :] = kvn


def _swa_sample(q, kvn, buf, sinks):
    b, l, _ = q.shape
    bb = _tile(b, 8)
    return pl.pallas_call(
        functools.partial(_swa_sample_kernel, bb=bb, l=l),
        grid=(b // bb,),
        in_specs=[pl.BlockSpec(memory_space=pltpu.SMEM),
                  pl.BlockSpec((bb, l, A_QW), lambda i: (i, 0, 0)),
                  pl.BlockSpec((bb, l, 2 * A_KVW), lambda i: (i, 0, 0)),
                  pl.BlockSpec((bb, WINDOW, 2 * A_KVW), lambda i: (i, 0, 0))],
        out_specs=[pl.BlockSpec((bb, l, A_QW), lambda i: (i, 0, 0)),
                   pl.BlockSpec((bb, WINDOW, 2 * A_KVW), lambda i: (i, 0, 0))],
        out_shape=[jax.ShapeDtypeStruct((b, l, A_QW), BF16),
                   jax.ShapeDtypeStruct((b, WINDOW, 2 * A_KVW), F32)],
        compiler_params=_params("arbitrary"),
        name="swa_sample",
    )(sinks, q, kvn, buf)


def _ssd_chunk(conv, z, dt, state, a_row, dsk, gn):
    L = SSD_CHUNK
    gw = D_INNER // B_GROUPS
    hpg = B_HEADS // B_GROUPS
    xs = conv[:, :D_INNER]
    adt = a_row * dt
    r2 = lax.broadcasted_iota(jnp.int32, (L, L), 0)
    c2 = lax.broadcasted_iota(jnp.int32, (L, L), 1)
    lower = r2 >= c2
    acs = jnp.dot(lower.astype(F32), adt, precision=_HI, preferred_element_type=F32)
    acs_t = acs.T
    band = lax.broadcasted_iota(jnp.int32, (L, gw), 1) // B_HEAD_DIM

    ys, states = [], []
    for g in range(B_GROUPS):
        b_g = conv[:, D_INNER + D_STATE * g:D_INNER + D_STATE * (g + 1)].astype(BF16)
        c_g = conv[:, D_INNER + B_GROUPS * D_STATE + D_STATE * g:
                   D_INNER + B_GROUPS * D_STATE + D_STATE * (g + 1)].astype(BF16)
        xs_g = xs[:, gw * g:gw * (g + 1)]

        def expand(mat, g=g):
            out = jnp.broadcast_to(mat[:, hpg * g:hpg * g + 1], (L, gw))
            for r in range(1, hpg):
                out = jnp.where(band == r, jnp.broadcast_to(mat[:, hpg * g + r:hpg * g + r + 1], (L, gw)), out)
            return out

        acs_e = expand(acs)
        xd = xs_g * expand(dt)
        xd_b = xd.astype(BF16)
        gmat = lax.dot_general(c_g, b_g, _NT, preferred_element_type=F32)
        y_diag = None
        for r in range(hpg):
            h = hpg * g + r
            diff = acs[:, h:h + 1] - acs_t[h:h + 1, :]
            lmat = jnp.exp(jnp.where(lower, diff, -jnp.inf))
            yr = jnp.dot((gmat * lmat).astype(BF16), xd_b, preferred_element_type=F32)
            y_diag = yr if y_diag is None else jnp.where(band == r, yr, y_diag)
        s_g = state[gw * g:gw * (g + 1), :]
        y_off = lax.dot_general(c_g, s_g.astype(BF16), _NT, preferred_element_type=F32) * jnp.exp(acs_e)
        decay = jnp.exp(acs_e[L - 1:L, :] - acs_e)
        upd = lax.dot_general((xd * decay).astype(BF16), b_g, _TN, preferred_element_type=F32)
        dec_rows = jnp.concatenate(
            [jnp.broadcast_to(jnp.exp(acs_t[hpg * g + r:hpg * g + r + 1, L - 1:L]), (B_HEAD_DIM, D_STATE))
             for r in range(hpg)], axis=0)
        states.append(s_g * dec_rows + upd)
        ys.append(y_diag + y_off + dsk[:, gw * g:gw * (g + 1)] * xs_g)

    outs = []
    for g in range(B_GROUPS):
        gt = ys[g] * _silu(z[:, gw * g:gw * (g + 1)])
        outs.append(_rms(gt))
    y = jnp.concatenate(outs, axis=-1) * gn
    return y, jnp.concatenate(states, axis=0)


def _conv_from_pad(xp_ref, cw_ref, cb_ref):
    L = SSD_CHUNK
    acc = cb_ref[...] + cw_ref[D_CONV - 1:D_CONV, :] * xp_ref[8:8 + L, :]
    for j in range(D_CONV - 1):
        acc = acc + cw_ref[j:j + 1, :] * xp_ref[5 + j:5 + j + L, :]
    return _silu(acc)


def _ssd_prompt_kernel(xbc_ref, z_ref, dt_ref, cw_ref, cb_ref, dtb_ref, a_ref, dsk_ref, gn_ref,
                       y_ref, st_ref, xp_ref, s_ref):
    i = pl.program_id(0)
    L = SSD_CHUNK

    @pl.when(i == 0)
    def _():
        xp_ref[0:8, :] = jnp.zeros((8, XBC_DIM), F32)
        s_ref[...] = jnp.zeros_like(s_ref)

    xp_ref[8:8 + L, :] = xbc_ref[...]
    conv = _conv_from_pad(xp_ref, cw_ref, cb_ref)
    xp_ref[0:8, :] = xp_ref[L:L + 8, :]
    dt = _softplus(dt_ref[...] + dtb_ref[...])
    y, new_state = _ssd_chunk(conv, z_ref[...], dt, s_ref[...], a_ref[...], dsk_ref[...], gn_ref[...])
    s_ref[...] = new_state
    y_ref[...] = y.astype(BF16)

    @pl.when(i == pl.num_programs(0) - 1)
    def _():
        st_ref[...] = new_state


def _ssd_prompt(xbc, z, dt, cw, cb, dtb, a_row, dsk, gn):
    t = xbc.shape[0]
    L = SSD_CHUNK
    consts = [cw, cb, dtb, a_row, dsk, gn]
    return pl.pallas_call(
        _ssd_prompt_kernel,
        grid=(t // L,),
        in_specs=[pl.BlockSpec((L, XBC_DIM), lambda i: (i, 0)),
                  pl.BlockSpec((L, D_INNER), lambda i: (i, 0)),
                  pl.BlockSpec((L, LANE), lambda i: (i, 0))] + [_const_spec(c) for c in consts],
        out_specs=[pl.BlockSpec((L, D_INNER), lambda i: (i, 0)),
                   pl.BlockSpec((B_HEADS * B_HEAD_DIM, D_STATE), lambda i: (0, 0))],
        out_shape=[jax.ShapeDtypeStruct((t, D_INNER), BF16),
                   jax.ShapeDtypeStruct((B_HEADS * B_HEAD_DIM, D_STATE), F32)],
        scratch_shapes=[pltpu.VMEM((L + 8, XBC_DIM), F32), pltpu.VMEM((B_HEADS * B_HEAD_DIM, D_STATE), F32)],
        compiler_params=_params("arbitrary"),
        name="ssd_prompt",
    )(xbc, z, dt, *consts)


def _ssd_sample_kernel(xbc_ref, z_ref, dt_ref, cs_ref, s0_ref, cw_ref, cb_ref, dtb_ref, a_ref, dsk_ref, gn_ref,
                       y_ref, st_ref, xp_ref, *, l):
    i = pl.program_id(0)
    L = SSD_CHUNK

    @pl.when(i == 0)
    def _():
        xp_ref[...] = jnp.zeros_like(xp_ref)

    xp_ref[5:8, :] = cs_ref[0]
    xp_ref[8:8 + l, :] = xbc_ref[0]
    conv = _conv_from_pad(xp_ref, cw_ref, cb_ref)
    pad = jnp.zeros((L - l, LANE), F32)
    dt = jnp.concatenate([_softplus(dt_ref[0] + dtb_ref[...]), pad], axis=0)
    z = jnp.concatenate([z_ref[0], jnp.zeros((L - l, D_INNER), F32)], axis=0)
    y, new_state = _ssd_chunk(conv, z, dt, s0_ref[0], a_ref[...], dsk_ref[...], gn_ref[...])
    y_ref[0] = y[0:l, :].astype(BF16)
    st_ref[0] = new_state


def _ssd_sample(xbc, z, dt, conv_state, ssm_state, cw, cb, dtb, a_row, dsk, gn):
    b, l, _ = xbc.shape
    L = SSD_CHUNK
    consts = [cw, cb, dtb, a_row, dsk, gn]
    hp = B_HEADS * B_HEAD_DIM
    return pl.pallas_call(
        functools.partial(_ssd_sample_kernel, l=l),
        grid=(b,),
        in_specs=[pl.BlockSpec((1, l, XBC_DIM), lambda i: (i, 0, 0)),
                  pl.BlockSpec((1, l, D_INNER), lambda i: (i, 0, 0)),
                  pl.BlockSpec((1, l, LANE), lambda i: (i, 0, 0)),
                  pl.BlockSpec((1, D_CONV - 1, XBC_DIM), lambda i: (i, 0, 0)),
                  pl.BlockSpec((1, hp, D_STATE), lambda i: (i, 0, 0))] + [_const_spec(c) for c in consts],
        out_specs=[pl.BlockSpec((1, l, D_INNER), lambda i: (i, 0, 0)),
                   pl.BlockSpec((1, hp, D_STATE), lambda i: (i, 0, 0))],
        out_shape=[jax.ShapeDtypeStruct((b, l, D_INNER), BF16),
                   jax.ShapeDtypeStruct((b, hp, D_STATE), F32)],
        scratch_shapes=[pltpu.VMEM((L + 8, XBC_DIM), F32)],
        compiler_params=_params("arbitrary"),
        name="ssd_sample",
    )(xbc, z, dt, conv_state, ssm_state, *consts)


def _proj_res_kernel(*refs, n_in):
    a_refs = refs[:n_in]
    x_ref, g_ref = refs[n_in:n_in + 2]
    w_refs = refs[n_in + 2:2 * n_in + 2]
    o_ref = refs[-1]
    acc = None
    for a_ref, w_ref in zip(a_refs, w_refs):
        part = jnp.dot(a_ref[...], w_ref[...], preferred_element_type=F32)
        acc = part if acc is None else acc + part
    o_ref[...] = x_ref[...] + g_ref[...] * acc


def _proj_res(acts, x, gate, ws):
    t, d = x.shape
    tm = _tile(t, 512)
    n_in = len(acts)
    return pl.pallas_call(
        functools.partial(_proj_res_kernel, n_in=n_in),
        grid=(t // tm,),
        in_specs=[pl.BlockSpec((tm, a.shape[1]), lambda i: (i, 0)) for a in acts]
        + [pl.BlockSpec((tm, d), lambda i: (i, 0)), _row_spec(gate, tm)] + [_const_spec(w) for w in ws],
        out_specs=pl.BlockSpec((tm, d), lambda i: (i, 0)),
        out_shape=jax.ShapeDtypeStruct((t, d), F32),
        compiler_params=_params("arbitrary"),
        name="proj_residual",
    )(*acts, x, gate, *ws)


def _swiglu_kernel(x_ref, sh_ref, sc_ref, g_ref, wgu_ref, wd_ref, o_ref, *, n_chunk):
    x = x_ref[...]
    h = (_rms(x) * (1.0 + sc_ref[...]) + sh_ref[...]).astype(BF16)
    f = wd_ref.shape[0]
    tf = f // n_chunk
    acc = None
    for c in range(n_chunk):
        gp = jnp.dot(h, wgu_ref[:, c * tf:(c + 1) * tf], preferred_element_type=F32)
        up = jnp.dot(h, wgu_ref[:, f + c * tf:f + (c + 1) * tf], preferred_element_type=F32)
        a = (_silu(gp) * up).astype(BF16)
        part = jnp.dot(a, wd_ref[c * tf:(c + 1) * tf, :], preferred_element_type=F32)
        acc = part if acc is None else acc + part
    o_ref[...] = x + g_ref[...] * acc


def _swiglu(x, sh, sc, gate, wgu, wd):
    t, d = x.shape
    tm = _tile(t, 512)
    return pl.pallas_call(
        functools.partial(_swiglu_kernel, n_chunk=2),
        grid=(t // tm,),
        in_specs=[pl.BlockSpec((tm, d), lambda i: (i, 0)), _row_spec(sh, tm), _row_spec(sc, tm), _row_spec(gate, tm),
                  _const_spec(wgu), _const_spec(wd)],
        out_specs=pl.BlockSpec((tm, d), lambda i: (i, 0)),
        out_shape=jax.ShapeDtypeStruct((t, d), F32),
        compiler_params=_params("arbitrary"),
        name="swiglu",
    )(x, sh, sc, gate, wgu, wd)


def _rope_lanes(blk, t_ref):
    half = C_ROPE // 2
    return (blk * t_ref[0] + pltpu.roll(blk, half, 1) * t_ref[1]
            + pltpu.roll(blk, LANE - half, 1) * t_ref[2])


def _od_latents(x_ref, sh_ref, sc_ref, w_in_ref, qn_ref, kvn_ref):
    x = x_ref[...]
    h = (_rms(x) * (1.0 + sc_ref[...]) + sh_ref[...]).astype(BF16)
    y = jnp.dot(h, w_in_ref[...], preferred_element_type=F32)
    cqn = (_rms(y[:, :Q_LORA]) * qn_ref[...]).astype(BF16)
    ckvn = _rms(y[:, Q_LORA:Q_LORA + KV_LORA]) * kvn_ref[...]
    return cqn, ckvn, y[:, Q_LORA + KV_LORA:]


def _od_in_prompt_kernel(x_ref, sh_ref, sc_ref, w_in_ref, qn_ref, kvn_ref, wuq_ref, wuk_ref, wuv_ref,
                         tq_ref, tk_ref, q_out, k_out, v_out, ckv_out, kpe_out):
    cqn, ckvn, kpe_pad = _od_latents(x_ref, sh_ref, sc_ref, w_in_ref, qn_ref, kvn_ref)
    ckv_out[...] = ckvn
    ckb = ckvn.astype(BF16)
    qf = jnp.dot(cqn, wuq_ref[...], preferred_element_type=F32)
    kf = jnp.dot(ckb, wuk_ref[...], preferred_element_type=F32)
    v_out[0] = lax.dot_general(wuv_ref[...], ckb, _NT, preferred_element_type=F32).astype(BF16)
    kpr = _rope_lanes(kpe_pad, tk_ref)
    kpe_out[...] = kpr[:, C_NOPE:C_NOPE + C_ROPE]
    for hh in range(C_HEADS):
        q_out[hh] = _rope_lanes(qf[:, LANE * hh:LANE * (hh + 1)], tq_ref).astype(BF16)
        k_out[hh] = (kf[:, LANE * hh:LANE * (hh + 1)] + kpr).astype(BF16)


def _od_in_prompt(x, sh, sc, w_in, qn, kvn, wuq, wuk, wuv, tq, tk):
    t, d = x.shape
    tm = _tile(t, MLA_TK)
    consts = [w_in, qn, kvn, wuq, wuk, wuv]
    return pl.pallas_call(
        _od_in_prompt_kernel,
        grid=(t // tm,),
        in_specs=[pl.BlockSpec((tm, d), lambda i: (i, 0)), _row_spec(sh, tm), _row_spec(sc, tm)]
        + [_const_spec(c) for c in consts]
        + [pl.BlockSpec((3, tm, LANE), lambda i: (0, i, 0)), pl.BlockSpec((3, tm, LANE), lambda i: (0, i, 0))],
        out_specs=[pl.BlockSpec((C_HEADS, tm, LANE), lambda i: (0, i, 0)),
                   pl.BlockSpec((C_HEADS, tm, LANE), lambda i: (0, i, 0)),
                   pl.BlockSpec((1, C_HEADS * C_V, tm), lambda i: (i, 0, 0)),
                   pl.BlockSpec((tm, KV_LORA), lambda i: (i, 0)),
                   pl.BlockSpec((tm, C_ROPE), lambda i: (i, 0))],
        out_shape=[jax.ShapeDtypeStruct((C_HEADS, t, LANE), BF16),
                   jax.ShapeDtypeStruct((C_HEADS, t, LANE), BF16),
                   jax.ShapeDtypeStruct((t // tm, C_HEADS * C_V, tm), BF16),
                   jax.ShapeDtypeStruct((t, KV_LORA), F32),
                   jax.ShapeDtypeStruct((t, C_ROPE), F32)],
        compiler_params=_params("arbitrary"),
        name="mla_proj_prompt",
    )(x, sh, sc, *consts, tq, tk)


def _od_in_sample_kernel(x_ref, sh_ref, sc_ref, w_in_ref, qn_ref, kvn_ref, wuq_ref, wukt_ref,
                         tq_ref, tk_ref, ql_out, qp_out, ckv_out, kpe_out, kpp_out):
    cqn, ckvn, kpe_pad = _od_latents(x_ref, sh_ref, sc_ref, w_in_ref, qn_ref, kvn_ref)
    ckv_out[...] = ckvn
    kpr = _rope_lanes(kpe_pad, tk_ref)
    kpe_out[...] = kpr[:, 0:C_ROPE]
    kpp_out[...] = kpr
    qf = jnp.dot(cqn, wuq_ref[...], preferred_element_type=F32)
    nq = C_HEADS * LANE
    for hh in range(C_HEADS):
        qn_h = (qf[:, LANE * hh:LANE * (hh + 1)] * MLA_SCALE).astype(BF16)
        ql = jnp.dot(qn_h, wukt_ref[hh], preferred_element_type=F32)
        ql_out[:, KV_LORA * hh:KV_LORA * (hh + 1)] = ql.astype(BF16)
        qp_out[:, LANE * hh:LANE * (hh + 1)] = _rope_lanes(qf[:, nq + LANE * hh:nq + LANE * (hh + 1)], tq_ref).astype(BF16)


def _od_in_sample(x, sh, sc, w_in, qn, kvn, wuq, wukt, tq, tk):
    t, d = x.shape
    tm = _tile(t, 512)
    consts = [w_in, qn, kvn, wuq, wukt]
    widths = [(C_HEADS * KV_LORA, BF16), (C_HEADS * LANE, BF16), (KV_LORA, F32), (C_ROPE, F32), (LANE, F32)]
    return pl.pallas_call(
        _od_in_sample_kernel,
        grid=(t // tm,),
        in_specs=[pl.BlockSpec((tm, d), lambda i: (i, 0)), _row_spec(sh, tm), _row_spec(sc, tm)]
        + [_const_spec(c) for c in consts]
        + [pl.BlockSpec((3, tm, LANE), lambda i: (0, i, 0)), pl.BlockSpec((3, tm, LANE), lambda i: (0, i, 0))],
        out_specs=[pl.BlockSpec((tm, n), lambda i: (i, 0)) for n, _ in widths],
        out_shape=[jax.ShapeDtypeStruct((t, n), dt) for n, dt in widths],
        compiler_params=_params("arbitrary"),
        name="mla_proj_sample",
    )(x, sh, sc, *consts, tq, tk)


def _flash_kernel(q_ref, k_ref, vt_ref, o_ref, m_ref, l_ref, acc_ref, sa_ref, mxa_ref, sb_ref, mxb_ref, *, tq, tk):
    qi = pl.program_id(1)
    m_ref[...] = jnp.full(m_ref.shape, -jnp.inf, F32)
    l_ref[...] = jnp.zeros(l_ref.shape, F32)
    acc_ref[...] = jnp.zeros(acc_ref.shape, F32)

    def scores(j, a, masked):
        start = pl.multiple_of(j * tk, tk)
        st = lax.dot_general(k_ref[a, pl.ds(start, tk), :], q_ref[a], _NT, preferred_element_type=F32)
        if masked:
            krow = lax.broadcasted_iota(jnp.int32, (tk, tq), 0) + (j * tk - qi * tq)
            qcol = lax.broadcasted_iota(jnp.int32, (tk, tq), 1)
            st = jnp.where(krow <= qcol, st, -jnp.inf)
        return st

    def absorb(j, a, st, mx):
        m_prev = m_ref[a]
        m_new = jnp.maximum(m_prev, mx)
        alpha = jnp.exp2(m_prev - m_new)
        p = jnp.exp2(st - m_new)
        l_ref[a] = alpha * l_ref[a] + jnp.sum(p, axis=0, keepdims=True)
        pv = jnp.dot(vt_ref[j, C_V * a:C_V * (a + 1), :], p.astype(BF16), preferred_element_type=F32)
        acc_ref[a] = alpha * acc_ref[a] + pv
        m_ref[a] = m_new

    n_full = (qi * tq) // tk
    for dd in range(max(tq // tk, 1)):
        for a in range(2):
            st = scores(n_full + dd, a, True)
            absorb(n_full + dd, a, st, jnp.max(st, axis=0, keepdims=True))

    def produce(j, buf):
        s_ref, mx_ref = buf
        for a in range(2):
            st = scores(j, a, False)
            s_ref[a] = st
            mx_ref[a] = jnp.max(st, axis=0, keepdims=True)

    def consume(j, buf):
        s_ref, mx_ref = buf
        for a in range(2):
            absorb(j, a, s_ref[a], mx_ref[a])

    buf_a, buf_b = (sa_ref, mxa_ref), (sb_ref, mxb_ref)

    @pl.when(n_full > 0)
    def _():
        produce(0, buf_a)
        n_loop = (n_full - 1) // 2

        def body(k, carry):
            produce(2 * k + 1, buf_b)
            consume(2 * k, buf_a)
            produce(2 * k + 2, buf_a)
            consume(2 * k + 1, buf_b)
            return carry

        lax.fori_loop(0, n_loop, body, 0)
        j0 = 2 * n_loop

        @pl.when(n_full - j0 == 2)
        def _():
            produce(j0 + 1, buf_b)
            consume(j0, buf_a)
            consume(j0 + 1, buf_b)

        @pl.when(n_full - j0 == 1)
        def _():
            consume(j0, buf_a)
    o_t = jnp.concatenate([acc_ref[0] / l_ref[0], acc_ref[1] / l_ref[1]], axis=0)
    o_ref[...] = o_t.T.astype(BF16)


def _flash_prompt(qh, kh, vt):
    nk, _, tk = vt.shape
    t = nk * tk
    tq = _tile(t, MLA_TQ)
    return pl.pallas_call(
        functools.partial(_flash_kernel, tq=tq, tk=tk),
        grid=(C_HEADS // 2, t // tq),
        in_specs=[pl.BlockSpec((2, tq, LANE), lambda hp, qi: (hp, qi, 0)),
                  pl.BlockSpec((2, t, LANE), lambda hp, qi: (hp, 0, 0)),
                  pl.BlockSpec((nk, 2 * C_V, tk), lambda hp, qi: (0, hp, 0))],
        out_specs=pl.BlockSpec((tq, LANE), lambda hp, qi: (qi, hp)),
        out_shape=jax.ShapeDtypeStruct((t, C_HEADS * C_V), BF16),
        scratch_shapes=[pltpu.VMEM((2, 1, tq), F32), pltpu.VMEM((2, 1, tq), F32), pltpu.VMEM((2, C_V, tq), F32),
                        pltpu.VMEM((2, tk, tq), F32), pltpu.VMEM((2, 1, tq), F32),
                        pltpu.VMEM((2, tk, tq), F32), pltpu.VMEM((2, 1, tq), F32)],
        compiler_params=_params("arbitrary", "arbitrary"),
        name="mla_flash_prompt",
    )(qh, kh, vt)


def _paged_kernel(pt_ref, ql_ref, qp_ref, cn_ref, kn_ref, ckv_hbm, kpt_hbm, o_ref,
                  ck_buf, kp_buf, ck_sem, kp_sem, ckb_ref, s_ref, mx_ref, m_ref, l_ref, acc_ref, *, ch, l):
    b = pl.program_id(0)
    nb = pl.num_programs(0)
    n_chunks = pt_ref.shape[1] // ch
    nq = ql_ref.shape[1]

    def ck_copy(bi, c, p, slot):
        page = pt_ref[bi, c * ch + p]
        return pltpu.make_async_copy(ckv_hbm.at[0, page], ck_buf.at[slot, pl.ds(p * PAGE_SIZE, PAGE_SIZE), :],
                                     ck_sem.at[slot])

    def kp_copy(bi, c, p, slot):
        page = pt_ref[bi, c * ch + p]
        return pltpu.make_async_copy(kpt_hbm.at[0, page], kp_buf.at[slot, 0:C_ROPE, pl.ds(p * PAGE_SIZE, PAGE_SIZE)],
                                     kp_sem.at[slot])

    def start_chunk(bi, c, slot):
        for p in range(ch):
            ck_copy(bi, c, p, slot).start()
            kp_copy(bi, c, p, slot).start()

    def wait_chunk(slot):
        for p in range(ch):
            ck_copy(0, 0, p, slot).wait()
            kp_copy(0, 0, p, slot).wait()

    @pl.when(b == 0)
    def _():
        kp_buf[...] = jnp.zeros_like(kp_buf)
        start_chunk(0, 0, 0)
        start_chunk(0, 1, 1)

    m_ref[...] = jnp.full(m_ref.shape, -jnp.inf, F32)
    l_ref[...] = jnp.zeros(l_ref.shape, F32)
    acc_ref[...] = jnp.zeros(acc_ref.shape, F32)
    ql = ql_ref[0]
    qp = qp_ref[0]

    def absorb(s, mx, values):
        m_prev = m_ref[...]
        m_new = jnp.maximum(m_prev, mx)
        alpha = jnp.exp(m_prev - m_new)
        p = jnp.exp(s - m_new)
        l_ref[...] = alpha * l_ref[...] + jnp.sum(p, axis=-1, keepdims=True)
        acc_ref[...] = alpha * acc_ref[...] + jnp.dot(p.astype(BF16), values, preferred_element_type=F32)
        m_ref[...] = m_new

    def produce(slot):
        ck = ck_buf[slot].astype(BF16)
        kp = kp_buf[slot].astype(BF16)
        s = lax.dot_general(ql, ck, _NT, preferred_element_type=F32) + jnp.dot(qp, kp, preferred_element_type=F32)
        ckb_ref[slot] = ck
        s_ref[slot] = s
        mx_ref[slot] = jnp.max(s, axis=-1, keepdims=True)

    def consume(slot):
        absorb(s_ref[slot], mx_ref[slot], ckb_ref[slot])

    wait_chunk(0)
    produce(0)

    def pair(k, carry):
        start_chunk(b, 2 * k + 2, 0)
        wait_chunk(1)
        produce(1)
        consume(0)
        start_chunk(b, 2 * k + 3, 1)
        wait_chunk(0)
        produce(0)
        consume(1)
        return carry

    lax.fori_loop(0, n_chunks // 2 - 1, pair, 0)

    @pl.when(b + 1 < nb)
    def _():
        start_chunk(b + 1, 0, 0)

    wait_chunk(1)
    produce(1)
    consume(0)

    @pl.when(b + 1 < nb)
    def _():
        start_chunk(b + 1, 1, 1)

    consume(1)

    cn = jnp.concatenate([cn_ref[0], jnp.zeros((LANE - l, KV_LORA), F32)], axis=0).astype(BF16)
    kn_t = jnp.concatenate([kn_ref[0], jnp.zeros((LANE - l, LANE), F32)], axis=0).T.astype(BF16)
    s = lax.dot_general(ql, cn, _NT, preferred_element_type=F32) + jnp.dot(qp, kn_t, preferred_element_type=F32)
    qtok = lax.broadcasted_iota(jnp.int32, (nq, LANE), 0) // C_HEADS
    kcol = lax.broadcasted_iota(jnp.int32, (nq, LANE), 1)
    s = jnp.where(kcol <= qtok, s, -jnp.inf)
    absorb(s, jnp.max(s, axis=-1, keepdims=True), cn)
    o_ref[0] = (acc_ref[...] / l_ref[...]).astype(BF16)


def _paged_attention(page_table, ql, qp, ckv_new, kpe_new_pad, cache_ckv, cache_kpe_t):
    b, nq, _ = ql.shape
    l = ckv_new.shape[1]
    n_pages = page_table.shape[1]
    ch = max(c for c in (16, 8, 4, 2, 1) if n_pages % (2 * c) == 0)
    grid_spec = pltpu.PrefetchScalarGridSpec(
        num_scalar_prefetch=1,
        grid=(b,),
        in_specs=[pl.BlockSpec((1, nq, KV_LORA), lambda bi, pt: (bi, 0, 0)),
                  pl.BlockSpec((1, nq, LANE), lambda bi, pt: (bi, 0, 0)),
                  pl.BlockSpec((1, l, KV_LORA), lambda bi, pt: (bi, 0, 0)),
                  pl.BlockSpec((1, l, LANE), lambda bi, pt: (bi, 0, 0)),
                  pl.BlockSpec(memory_space=pl.ANY),
                  pl.BlockSpec(memory_space=pl.ANY)],
        out_specs=pl.BlockSpec((1, nq, KV_LORA), lambda bi, pt: (bi, 0, 0)),
        scratch_shapes=[pltpu.VMEM((2, ch * PAGE_SIZE, KV_LORA), F32),
                        pltpu.VMEM((2, LANE, ch * PAGE_SIZE), F32),
                        pltpu.SemaphoreType.DMA((2,)),
                        pltpu.SemaphoreType.DMA((2,)),
                        pltpu.VMEM((2, ch * PAGE_SIZE, KV_LORA), BF16),
                        pltpu.VMEM((2, nq, ch * PAGE_SIZE), F32),
                        pltpu.VMEM((2, nq, 1), F32),
                        pltpu.VMEM((nq, 1), F32), pltpu.VMEM((nq, 1), F32), pltpu.VMEM((nq, KV_LORA), F32)],
    )
    return pl.pallas_call(
        functools.partial(_paged_kernel, ch=ch, l=l),
        grid_spec=grid_spec,
        out_shape=jax.ShapeDtypeStruct((b, nq, KV_LORA), BF16),
        compiler_params=_params("arbitrary"),
        name="mla_paged_sample",
    )(page_table, ql, qp, ckv_new, kpe_new_pad, cache_ckv, cache_kpe_t)


def _mla_out_sample_kernel(o_ref, x_ref, g_ref, wuv_ref, wo_ref, out_ref):
    parts = []
    for pr in range(C_HEADS // 2):
        parts.append(jnp.dot(o_ref[:, 2 * KV_LORA * pr:2 * KV_LORA * (pr + 1)], wuv_ref[pr],
                             preferred_element_type=F32))
    attn = jnp.concatenate(parts, axis=-1).astype(BF16)
    out_ref[...] = x_ref[...] + g_ref[...] * jnp.dot(attn, wo_ref[...], preferred_element_type=F32)


def _mla_out_sample(o_lat, x, gate, wuv_pairs, wo):
    t, d = x.shape
    tm = _tile(t, 512)
    return pl.pallas_call(
        _mla_out_sample_kernel,
        grid=(t // tm,),
        in_specs=[pl.BlockSpec((tm, o_lat.shape[1]), lambda i: (i, 0)), pl.BlockSpec((tm, d), lambda i: (i, 0)),
                  _row_spec(gate, tm), _const_spec(wuv_pairs), _const_spec(wo)],
        out_specs=pl.BlockSpec((tm, d), lambda i: (i, 0)),
        out_shape=jax.ShapeDtypeStruct((t, d), F32),
        compiler_params=_params("arbitrary"),
        name="mla_out_sample",
    )(o_lat, x, gate, wuv_pairs, wo)


def _route_kernel(x_ref, sh_ref, sc_ref, r_ref, h_ref, gate_ref, idx_ref):
    x = x_ref[...]
    h = _rms(x) * (1.0 + sc_ref[...]) + sh_ref[...]
    h_ref[...] = h
    logits = jnp.dot(h, r_ref[...], precision=_HI, preferred_element_type=F32)
    lane = lax.broadcasted_iota(jnp.int32, logits.shape, 1)
    logits = jnp.where(lane < N_EXPERTS, logits, -jnp.inf)
    v1 = jnp.max(logits, axis=-1, keepdims=True)
    i1 = jnp.min(jnp.where(logits == v1, lane, LANE), axis=-1, keepdims=True)
    rest = jnp.where(lane == i1, -jnp.inf, logits)
    v2 = jnp.max(rest, axis=-1, keepdims=True)
    i2 = jnp.min(jnp.where(rest == v2, lane, LANE), axis=-1, keepdims=True)
    e2 = jnp.exp(v2 - v1)
    gate_ref[...] = jnp.where(lane == 0, 1.0 / (1.0 + e2), jnp.where(lane == 1, e2 / (1.0 + e2), 0.0))
    idx_ref[...] = jnp.where(lane == 0, i1, jnp.where(lane == 1, i2, 0))


def _route(x, sh, sc, router_pad):
    t, d = x.shape
    tm = _tile(t, 512)
    return pl.pallas_call(
        _route_kernel,
        grid=(t // tm,),
        in_specs=[pl.BlockSpec((tm, d), lambda i: (i, 0)), _row_spec(sh, tm), _row_spec(sc, tm), _const_spec(router_pad)],
        out_specs=[pl.BlockSpec((tm, d), lambda i: (i, 0)), pl.BlockSpec((tm, LANE), lambda i: (i, 0)),
                   pl.BlockSpec((tm, LANE), lambda i: (i, 0))],
        out_shape=[jax.ShapeDtypeStruct((t, d), F32), jax.ShapeDtypeStruct((t, LANE), F32),
                   jax.ShapeDtypeStruct((t, LANE), jnp.int32)],
        compiler_params=_params("arbitrary"),
        name="moe_route",
    )(x, sh, sc, router_pad)


def _moe_plan(idx, tm):
    t = idx.shape[0]
    n_tiles = -(-2 * t // tm) + N_EXPERTS
    e_flat = idx.reshape(-1)
    onehot = (e_flat[:, None] == jnp.arange(N_EXPERTS, dtype=jnp.int32)[None]).astype(jnp.int32)
    csum = jnp.cumsum(onehot, axis=0)
    rank = jnp.sum(csum * onehot, axis=1) - 1
    padded = (csum[-1] + tm - 1) // tm * tm
    gend = jnp.cumsum(padded)
    dest = (gend - padded)[e_flat] + rank
    row_token = jnp.zeros((n_tiles * tm,), jnp.int32).at[dest].set(jnp.arange(2 * t, dtype=jnp.int32) // 2)
    n_used = gend[-1] // tm
    tile_start = jnp.minimum(jnp.arange(n_tiles, dtype=jnp.int32), n_used - 1) * tm
    tile_expert = jnp.sum((gend[None, :] <= tile_start[:, None]).astype(jnp.int32), axis=1)
    return dest, row_token, tile_expert.astype(jnp.int32), n_used.reshape(1).astype(jnp.int32)


def _row_loop(n, fn):
    def body(r, carry):
        fn(r)
        return carry

    lax.fori_loop(0, n, body, 0, unroll=8)


def _moe_gather_kernel(rt_ref, h_hbm, xs_ref, sem, *, tm):
    i = pl.program_id(0)

    def row_copy(r, tok):
        return pltpu.make_async_copy(h_hbm.at[pl.ds(tok, 1)], xs_ref.at[pl.ds(r, 1)], sem.at[0])

    _row_loop(tm, lambda r: row_copy(r, rt_ref[i * tm + r]).start())
    _row_loop(tm, lambda r: row_copy(0, 0).wait())


def _moe_gather(row_token, h, tm):
    r = row_token.shape[0]
    d = h.shape[1]
    return pl.pallas_call(
        functools.partial(_moe_gather_kernel, tm=tm),
        grid_spec=pltpu.PrefetchScalarGridSpec(
            num_scalar_prefetch=1, grid=(r // tm,),
            in_specs=[pl.BlockSpec(memory_space=pl.ANY)],
            out_specs=pl.BlockSpec((tm, d), lambda i, rt: (i, 0)),
            scratch_shapes=[pltpu.SemaphoreType.DMA((1,))]),
        out_shape=jax.ShapeDtypeStruct((r, d), h.dtype),
        compiler_params=_params("arbitrary"),
        name="moe_gather",
    )(row_token, h)


def _moe_grouped_kernel(te_ref, nu_ref, xs_ref, wg_ref, wu_ref, wd_ref, y_ref, xb_ref):
    del te_ref
    i = pl.program_id(0)
    c = pl.program_id(1)

    @pl.when(c == 0)
    def _():
        xb_ref[...] = xs_ref[...].astype(BF16)
        y_ref[...] = jnp.zeros_like(y_ref)

    @pl.when(i < nu_ref[0])
    def _():
        xb = xb_ref[...]
        gp = jnp.dot(xb, wg_ref[0], preferred_element_type=F32)
        up = jnp.dot(xb, wu_ref[0], preferred_element_type=F32)
        y_ref[...] += jnp.dot((_silu(gp) * up).astype(BF16), wd_ref[0], preferred_element_type=F32)


def _moe_grouped(tile_expert, n_used, xs, wgu, wd, tm):
    r, d = xs.shape
    f = wd.shape[1]
    tf = max(c for c in (896, 512, 256, 128) if f % c == 0)
    nc = f // tf

    def chunk(i, c, nu):
        return jnp.where(i < nu[0], c, nc - 1)

    return pl.pallas_call(
        _moe_grouped_kernel,
        grid_spec=pltpu.PrefetchScalarGridSpec(
            num_scalar_prefetch=2, grid=(r // tm, nc),
            in_specs=[pl.BlockSpec((tm, d), lambda i, c, te, nu: (i, 0)),
                      pl.BlockSpec((1, d, tf), lambda i, c, te, nu: (te[i], 0, chunk(i, c, nu))),
                      pl.BlockSpec((1, d, tf), lambda i, c, te, nu: (te[i], 0, nc + chunk(i, c, nu))),
                      pl.BlockSpec((1, tf, d), lambda i, c, te, nu: (te[i], chunk(i, c, nu), 0))],
            out_specs=pl.BlockSpec((tm, d), lambda i, c, te, nu: (i, 0)),
            scratch_shapes=[pltpu.VMEM((tm, d), BF16)]),
        out_shape=jax.ShapeDtypeStruct((r, d), F32),
        compiler_params=_params("arbitrary", "arbitrary"),
        name="moe_grouped",
    )(tile_expert, n_used, xs, wgu, wgu, wd)


def _moe_combine_kernel(pos_ref, x_ref, g2_ref, gate_ref, fn_ref, y_hbm, o_ref, ya_ref, yb_ref, sem, *, tm):
    i = pl.program_id(0)
    slot = i % 2

    def row_copy(buf, s, r, src):
        return pltpu.make_async_copy(y_hbm.at[pl.ds(src, 1)], buf.at[s, pl.ds(r, 1)], sem.at[s])

    def issue(step, s):
        def one(r):
            t = step * tm + r
            row_copy(ya_ref, s, r, pos_ref[2 * t]).start()
            row_copy(yb_ref, s, r, pos_ref[2 * t + 1]).start()

        _row_loop(tm, one)

    @pl.when(i == 0)
    def _():
        issue(0, 0)

    @pl.when(i + 1 < pl.num_programs(0))
    def _():
        issue(i + 1, 1 - slot)

    def wait_one(r):
        row_copy(ya_ref, slot, 0, 0).wait()
        row_copy(yb_ref, slot, 0, 0).wait()

    _row_loop(tm, wait_one)
    gate = gate_ref[...]
    y = gate[:, 0:1] * ya_ref[slot] + gate[:, 1:2] * yb_ref[slot]
    xo = x_ref[...] + g2_ref[...] * y
    o_ref[...] = _rms(xo) * fn_ref[...]


def _moe_combine(pos, x, g2, gate, fn, y):
    t, d = x.shape
    tm = _tile(t, 256)
    return pl.pallas_call(
        functools.partial(_moe_combine_kernel, tm=tm),
        grid_spec=pltpu.PrefetchScalarGridSpec(
            num_scalar_prefetch=1, grid=(t // tm,),
            in_specs=[pl.BlockSpec((tm, d), lambda i, p: (i, 0)),
                      (pl.BlockSpec((1, d), lambda i, p: (0, 0)) if g2.shape[0] == 1
                       else pl.BlockSpec((tm, d), lambda i, p: (i, 0))),
                      pl.BlockSpec((tm, LANE), lambda i, p: (i, 0)),
                      pl.BlockSpec((1, d), lambda i, p: (0, 0)),
                      pl.BlockSpec(memory_space=pl.ANY)],
            out_specs=pl.BlockSpec((tm, d), lambda i, p: (i, 0)),
            scratch_shapes=[pltpu.VMEM((2, tm, d), F32), pltpu.VMEM((2, tm, d), F32), pltpu.SemaphoreType.DMA((2,))]),
        out_shape=jax.ShapeDtypeStruct((t, d), F32),
        compiler_params=_params("arbitrary"),
        name="moe_combine",
    )(pos, x, g2, gate, fn, y)


def _rope_tables(pos, lo, reps, scale):
    half = C_ROPE // 2
    freqs = ROPE_THETA ** (-2.0 * jnp.arange(half, dtype=F32) / C_ROPE)
    ang = pos.astype(F32)[:, None] * freqs[None]
    cos, sin = jnp.cos(ang), jnp.sin(ang)
    n = pos.shape[0]
    zeros = jnp.zeros((n, half), F32)
    c_grp = jnp.concatenate([cos, cos], axis=1)
    s1_grp = jnp.concatenate([zeros, sin], axis=1)
    s2_grp = jnp.concatenate([-sin, zeros], axis=1)

    def lay(grp, fill):
        body = jnp.tile(grp, (1, reps))
        left = jnp.full((n, lo), fill, F32)
        right = jnp.zeros((n, LANE - lo - reps * C_ROPE), F32)
        return jnp.concatenate([left, body, right], axis=1)

    return jnp.stack([lay(c_grp, 1.0), lay(s1_grp, 0.0), lay(s2_grp, 0.0)]) * scale


def _pad_cols(w, n):
    return jnp.pad(w, ((0, 0), (0, n - w.shape[1])))


def kernel(x_prompt, x_sample, c_prompt, c_sample, state_swa_kv, state_conv, state_ssm, cache_ckv, cache_kpe, page_table,
           ev_mod_w, ev_mod_b, ev_w_in, ev_sinks, ev_conv_w, ev_conv_b, ev_dt_bias, ev_a_log, ev_d_skip, ev_gnorm,
           ev_w_out, ev_w_gu, ev_w_down, od_mod_w, od_mod_b, od_w_in, od_qnorm, od_kvnorm, od_w_uq, od_w_uk, od_w_uv,
           od_w_out, od_router, od_w_gu, od_w_down, final_norm):
    d = D_MODEL
    _, tp, _ = x_prompt.shape
    bs, ls, _ = x_sample.shape
    ts = bs * ls
    n_pages = page_table.shape[1]
    past = n_pages * PAGE_SIZE
    assert state_swa_kv.shape[2] == WINDOW and ls <= 8 and ls >= D_CONV - 1

    xp = x_prompt.reshape(tp, d)
    xs = x_sample.reshape(ts, d)

    n_c = 1 + bs
    n_cp = -(-n_c // 8) * 8
    c_all = jnp.pad(jnp.concatenate([c_prompt, c_sample], axis=0), ((0, n_cp - n_c), (0, 0)))

    def mods(w, b):
        m = _ada_mod(c_all, w, b)
        mp = [m[0:1, k * d:(k + 1) * d] for k in range(MOD_SLOTS)]
        ms = [jnp.repeat(m[1:n_c, k * d:(k + 1) * d], ls, axis=0) for k in range(MOD_SLOTS)]
        return mp, ms

    i = 0
    mp, ms = mods(ev_mod_w[i], ev_mod_b[i])
    w_in = ev_w_in[i]
    o_k, o_v, o_z, o_x, o_dt = A_QW, A_QW + A_KVW, A_QW + 2 * A_KVW, A_QW + 2 * A_KVW + D_INNER, A_QW + 2 * A_KVW + D_INNER + XBC_DIM
    w_in_p = jnp.concatenate([w_in[:, :o_k], w_in[:, o_z:o_x], w_in[:, o_x:o_dt], w_in[:, o_k:o_z],
                              _pad_cols(w_in[:, o_dt:], LANE)], axis=1).astype(BF16)
    sinks = ev_sinks[i].astype(F32)
    cw = ev_conv_w[i]
    cb = ev_conv_b[i].reshape(1, XBC_DIM)
    dtb = _pad_cols(ev_dt_bias[i].reshape(1, B_HEADS).astype(F32), LANE)
    a_row = _pad_cols(-jnp.exp(ev_a_log[i].astype(F32)).reshape(1, B_HEADS), LANE)
    dsk = jnp.repeat(ev_d_skip[i].astype(F32), B_HEAD_DIM).reshape(1, D_INNER)
    gn = ev_gnorm[i].reshape(1, D_INNER)
    w_out = ev_w_out[i].astype(BF16)
    w_gu = ev_w_gu[i].astype(BF16)
    w_dn = ev_w_down[i].astype(BF16)
    ssd_consts = (cw, cb, dtb, a_row, dsk, gn)

    q, z, xbc, kv, dt = _ev_in(xp, mp[0], mp[1], w_in_p)
    attn = _swa_prompt(q, kv, sinks)
    ssm, ssm_state_p = _ssd_prompt(xbc, z, dt, *ssd_consts)
    xp = _proj_res([attn, ssm], xp, mp[2], [w_out[:A_QW], w_out[A_QW:]])
    xp = _swiglu(xp, mp[3], mp[4], mp[5], w_gu, w_dn)
    swa_kv_prompt = kv[tp - WINDOW:].reshape(1, 1, WINDOW, 2, A_KV_HEADS, A_HEAD_DIM)
    conv_prompt = xbc[tp - (D_CONV - 1):].reshape(1, 1, D_CONV - 1, XBC_DIM)
    ssm_prompt = ssm_state_p.reshape(1, 1, B_HEADS, B_HEAD_DIM, D_STATE)

    q, z, xbc, kv, dt = _ev_in(xs, ms[0], ms[1], w_in_p)
    buf = state_swa_kv[i].reshape(bs, WINDOW, 2 * A_KVW)
    attn, nbuf = _swa_sample(q.reshape(bs, ls, A_QW), kv.reshape(bs, ls, 2 * A_KVW), buf, sinks)
    xbc3 = xbc.reshape(bs, ls, XBC_DIM)
    ssm, ssm_state_s = _ssd_sample(xbc3, z.reshape(bs, ls, D_INNER), dt.reshape(bs, ls, LANE), state_conv[i],
                                   state_ssm[i].reshape(bs, B_HEADS * B_HEAD_DIM, D_STATE), *ssd_consts)
    xs = _proj_res([attn.reshape(ts, A_QW), ssm.reshape(ts, D_INNER)], xs, ms[2], [w_out[:A_QW], w_out[A_QW:]])
    xs = _swiglu(xs, ms[3], ms[4], ms[5], w_gu, w_dn)
    swa_kv_sample = nbuf.reshape(1, bs, WINDOW, 2, A_KV_HEADS, A_HEAD_DIM)
    conv_sample = xbc3[:, ls - (D_CONV - 1):].reshape(1, bs, D_CONV - 1, XBC_DIM)
    ssm_sample = ssm_state_s.reshape(1, bs, B_HEADS, B_HEAD_DIM, D_STATE)

    mp, ms = mods(od_mod_w[i], od_mod_b[i])
    w_in = od_w_in[i]
    w_cq, w_ckv, w_kpe = w_in[:, :Q_LORA], w_in[:, Q_LORA:Q_LORA + KV_LORA], w_in[:, Q_LORA + KV_LORA:]
    qn = od_qnorm[i].reshape(1, Q_LORA)
    kvn = od_kvnorm[i].reshape(1, KV_LORA)
    w_uq = od_w_uq[i].reshape(Q_LORA, C_HEADS, C_NOPE + C_ROPE)
    w_uk = od_w_uk[i]
    w_uv = od_w_uv[i]
    w_o = od_w_out[i].astype(BF16)

    w_in_pp = jnp.concatenate([w_cq, w_ckv, jnp.pad(w_kpe, ((0, 0), (C_NOPE, LANE - C_NOPE - C_ROPE)))], axis=1).astype(BF16)
    wuq_p = jnp.pad(w_uq, ((0, 0), (0, 0), (0, LANE - C_NOPE - C_ROPE))).reshape(Q_LORA, C_HEADS * LANE).astype(BF16)
    wuk_p = jnp.pad(w_uk, ((0, 0), (0, 0), (0, LANE - C_NOPE))).reshape(KV_LORA, C_HEADS * LANE).astype(BF16)
    wuv_p = w_uv.reshape(KV_LORA, C_HEADS * C_V).T.astype(BF16)
    pos_p = jnp.arange(tp, dtype=jnp.int32)
    tq_p = _rope_tables(pos_p, C_NOPE, 1, MLA_SCALE * LOG2E)
    tk_p = _rope_tables(pos_p, C_NOPE, 1, 1.0)
    qh, kh, v, ckv_p, kpe_p = _od_in_prompt(xp, mp[0], mp[1], w_in_pp, qn, kvn, wuq_p, wuk_p, wuv_p, tq_p, tk_p)
    attn = _flash_prompt(qh, kh, v)
    xp = _proj_res([attn], xp, mp[2], [w_o])

    w_in_ps = jnp.concatenate([w_cq, w_ckv, _pad_cols(w_kpe, LANE)], axis=1).astype(BF16)
    wq_nope = jnp.pad(w_uq[:, :, :C_NOPE], ((0, 0), (0, 0), (0, LANE - C_NOPE))).reshape(Q_LORA, C_HEADS * LANE)
    wq_rope = jnp.pad(w_uq[:, :, C_NOPE:], ((0, 0), (0, 0), (0, LANE - C_ROPE))).reshape(Q_LORA, C_HEADS * LANE)
    wuq_s = jnp.concatenate([wq_nope, wq_rope], axis=1).astype(BF16)
    wukt_s = jnp.pad(jnp.transpose(w_uk, (1, 2, 0)), ((0, 0), (0, LANE - C_NOPE), (0, 0))).astype(BF16)
    pos_s = jnp.tile(past + jnp.arange(ls, dtype=jnp.int32), bs)
    tq_s = _rope_tables(pos_s, 0, 1, MLA_SCALE)
    tk_s = _rope_tables(pos_s, 0, 1, 1.0)
    ql, qp, ckv_s, kpe_s, kpe_pad_s = _od_in_sample(xs, ms[0], ms[1], w_in_ps, qn, kvn, wuq_s, wukt_s, tq_s, tk_s)
    nq = ls * C_HEADS
    o_lat = _paged_attention(page_table, ql.reshape(bs, nq, KV_LORA), qp.reshape(bs, nq, LANE),
                             ckv_s.reshape(bs, ls, KV_LORA), kpe_pad_s.reshape(bs, ls, LANE),
                             cache_ckv[i:i + 1], jnp.swapaxes(cache_kpe[i:i + 1], 2, 3))
    wuv_h = jnp.transpose(w_uv, (1, 0, 2)).reshape(C_HEADS // 2, 2, KV_LORA, C_V)
    zero = jnp.zeros((C_HEADS // 2, KV_LORA, C_V), F32)
    wuv_pairs = jnp.concatenate([jnp.concatenate([wuv_h[:, 0], zero], axis=2),
                                 jnp.concatenate([zero, wuv_h[:, 1]], axis=2)], axis=1).astype(BF16)
    xs = _mla_out_sample(o_lat.reshape(ts, C_HEADS * KV_LORA), xs, ms[2], wuv_pairs, w_o)

    router_pad = _pad_cols(od_router[i].astype(F32), LANE)
    wgu_e = od_w_gu[i].astype(BF16)
    wdn_e = od_w_down[i].astype(BF16)
    fn = final_norm.reshape(1, d).astype(F32)
    h_p, gate_p, idx_p = _route(xp, mp[3], mp[4], router_pad)
    h_s, gate_s, idx_s = _route(xs, ms[3], ms[4], router_pad)
    idx_all = jnp.concatenate([idx_p[:, :2], idx_s[:, :2]], axis=0)
    dest, row_token, tile_expert, n_used = _moe_plan(idx_all, MOE_TM)
    xs_sorted = _moe_gather(row_token, jnp.concatenate([h_p, h_s], axis=0), MOE_TM)
    y_sorted = _moe_grouped(tile_expert, n_used, xs_sorted, wgu_e, wdn_e, MOE_TM)
    y_prompt = _moe_combine(dest[:2 * tp], xp, mp[5], gate_p, fn, y_sorted)
    y_sample = _moe_combine(dest[2 * tp:], xs, ms[5], gate_s, fn, y_sorted)

    return (y_prompt.reshape(1, tp, d), y_sample.reshape(bs, ls, d),
            swa_kv_prompt, swa_kv_sample, conv_prompt, conv_sample, ssm_prompt, ssm_sample,
            ckv_p.reshape(1, 1, tp, KV_LORA), ckv_s.reshape(1, bs, ls, KV_LORA),
            kpe_p.reshape(1, 1, tp, C_ROPE), kpe_s.reshape(1, bs, ls, C_ROPE))
```

```python
import functools
import math

import jax
import jax.numpy as jnp
from jax import lax
from jax.experimental import pallas as pl
from jax.experimental.pallas import tpu as pltpu

F32 = jnp.float32
BF16 = jnp.bfloat16

D_MODEL = 1024
EPS = 1e-6
MOD_SLOTS = 6

A_HEAD_DIM = 64
A_HEADS = 8
A_KV_HEADS = 2
A_REP = A_HEADS // A_KV_HEADS
WINDOW = 128
A_QW = A_HEADS * A_HEAD_DIM
A_KVW = A_KV_HEADS * A_HEAD_DIM

D_INNER = 512
B_HEAD_DIM = 64
B_HEADS = 8
B_GROUPS = 2
D_STATE = 128
D_CONV = 4
SSD_CHUNK = 128
XBC_DIM = D_INNER + 2 * B_GROUPS * D_STATE

C_HEADS = 16
C_NOPE = 64
C_ROPE = 32
C_V = 64
Q_LORA = 384
KV_LORA = 256
ROPE_THETA = 10000.0
MLA_SCALE = (C_NOPE + C_ROPE) ** -0.5
PAGE_SIZE = 128

D_FF = 2816
N_EXPERTS = 8
D_FF_EXPERT = 3584

LANE = 128
MLA_TK = 512
MLA_TQ = 512
LOG2E = 1.4426950408889634
MOE_TM = 512
VMEM_LIMIT = 56 * 1024 * 1024

_HI = lax.Precision.HIGHEST
_NT = (((1,), (1,)), ((), ()))
_TN = (((0,), (0,)), ((), ()))


def _params(*sem):
    return pltpu.CompilerParams(dimension_semantics=sem, vmem_limit_bytes=VMEM_LIMIT)


def _tile(n, pref):
    t = min(n, pref)
    while n % t:
        t -= 8
    return t


def _rms(x):
    return x * lax.rsqrt(jnp.mean(x * x, axis=-1, keepdims=True) + EPS)


def _silu(x):
    return x * jax.nn.sigmoid(x)


def _softplus(x):
    e = jnp.exp(-jnp.abs(x))
    u = 1.0 + e
    lg = jnp.where(u == 1.0, e, jnp.log(u) * e / (u - 1.0))
    return jnp.maximum(x, 0.0) + lg


def _row_spec(arr, tm):
    d = arr.shape[1]
    if arr.shape[0] == 1:
        return pl.BlockSpec((1, d), lambda i: (0, 0))
    return pl.BlockSpec((tm, d), lambda i: (i, 0))


def _const_spec(arr):
    nd = arr.ndim
    return pl.BlockSpec(arr.shape, lambda *_: (0,) * nd)


def _mod_kernel(c_ref, w_ref, b_ref, o_ref):
    s = _silu(c_ref[...]).astype(BF16)
    o_ref[...] = jnp.dot(s, w_ref[...].astype(BF16), preferred_element_type=F32) + b_ref[...]


def _ada_mod(c_all, w, b):
    r, d = c_all.shape
    n = w.shape[1]
    tn = 1024
    return pl.pallas_call(
        _mod_kernel,
        grid=(n // tn,),
        in_specs=[pl.BlockSpec((r, d), lambda j: (0, 0)),
                  pl.BlockSpec((d, tn), lambda j: (0, j)),
                  pl.BlockSpec((1, tn), lambda j: (0, j))],
        out_specs=pl.BlockSpec((r, tn), lambda j: (0, j)),
        out_shape=jax.ShapeDtypeStruct((r, n), F32),
        compiler_params=_params("arbitrary"),
        name="ada_mod",
    )(c_all, w, b.reshape(1, n))


def _ev_in_kernel(x_ref, sh_ref, sc_ref, w_ref, q_ref, z_ref, xbc_ref, kv_ref, dt_ref):
    x = x_ref[...]
    h = _rms(x) * (1.0 + sc_ref[...]) + sh_ref[...]
    y = jnp.dot(h.astype(BF16), w_ref[...], preferred_element_type=F32)
    q_ref[...] = y[:, 0:A_QW].astype(BF16)
    z_ref[...] = y[:, A_QW:A_QW + D_INNER]
    xbc_ref[...] = y[:, 1024:1024 + XBC_DIM]
    kv_ref[...] = y[:, 2048:2048 + 2 * A_KVW]
    dt_ref[...] = y[:, 2304:2304 + LANE]


def _ev_in(x, sh, sc, w):
    t, d = x.shape
    tm = _tile(t, 512)
    outs = [(A_QW, BF16), (D_INNER, F32), (XBC_DIM, F32), (2 * A_KVW, F32), (LANE, F32)]
    return pl.pallas_call(
        _ev_in_kernel,
        grid=(t // tm,),
        in_specs=[pl.BlockSpec((tm, d), lambda i: (i, 0)), _row_spec(sh, tm), _row_spec(sc, tm), _const_spec(w)],
        out_specs=[pl.BlockSpec((tm, n), lambda i: (i, 0)) for n, _ in outs],
        out_shape=[jax.ShapeDtypeStruct((t, n), dt) for n, dt in outs],
        compiler_params=_params("arbitrary"),
        name="ev_in_proj",
    )(x, sh, sc, w)


def _swa_heads(q, k_all, v_all, valid, distf, sink_ref):
    lq = q.shape[0]
    rep = lax.broadcasted_iota(jnp.int32, (A_REP * lq, 1), 0) // lq
    valid_s = jnp.concatenate([valid] * A_REP, axis=0)
    dist_s = jnp.concatenate([distf] * A_REP, axis=0)

    def per_head(g, vals):
        col = jnp.full((A_REP * lq, 1), vals[g * A_REP], F32)
        for r in range(1, A_REP):
            col = jnp.where(rep == r, vals[g * A_REP + r], col)
        return col

    slopes = [2.0 ** (-8.0 * (h + 1) / A_HEADS) for h in range(A_HEADS)]
    sinks = [sink_ref[h] for h in range(A_HEADS)]
    outs = []
    for g in range(A_KV_HEADS):
        k_g = k_all[:, A_HEAD_DIM * g:A_HEAD_DIM * (g + 1)].astype(BF16)
        v_g = v_all[:, A_HEAD_DIM * g:A_HEAD_DIM * (g + 1)].astype(BF16)
        q_g = jnp.concatenate([q[:, A_HEAD_DIM * h:A_HEAD_DIM * (h + 1)]
                               for h in range(g * A_REP, (g + 1) * A_REP)], axis=0)
        s = lax.dot_general(q_g, k_g, _NT, preferred_element_type=F32) * (A_HEAD_DIM ** -0.5)
        s = s - per_head(g, slopes) * dist_s
        s = jnp.where(valid_s, s, -jnp.inf)
        sink = per_head(g, sinks)
        m = jnp.maximum(jnp.max(s, axis=-1, keepdims=True), sink)
        p = jnp.exp(s - m)
        den = jnp.sum(p, axis=-1, keepdims=True) + jnp.exp(sink - m)
        o = jnp.dot(p.astype(BF16), v_g, preferred_element_type=F32) / den
        outs.extend(o[r * lq:(r + 1) * lq] for r in range(A_REP))
    return jnp.concatenate(outs, axis=-1)


def _swa_prompt_kernel(sink_ref, q_ref, kvp_ref, kvc_ref, o_ref):
    i = pl.program_id(0)
    kv = jnp.concatenate([kvp_ref[...], kvc_ref[...]], axis=0)
    row = lax.broadcasted_iota(jnp.int32, (WINDOW, 2 * WINDOW), 0)
    col = lax.broadcasted_iota(jnp.int32, (WINDOW, 2 * WINDOW), 1)
    dist = row + WINDOW - col
    first_key = jnp.where(i > 0, 0, WINDOW)
    valid = (dist >= 0) & (dist < WINDOW) & (col >= first_key)
    o = _swa_heads(q_ref[...], kv[:, :A_KVW], kv[:, A_KVW:], valid, dist.astype(F32), sink_ref)
    o_ref[...] = o.astype(BF16)


def _swa_prompt(q, kv, sinks):
    t = q.shape[0]
    nb = t // WINDOW
    return pl.pallas_call(
        _swa_prompt_kernel,
        grid=(nb,),
        in_specs=[pl.BlockSpec(memory_space=pltpu.SMEM),
                  pl.BlockSpec((WINDOW, A_QW), lambda i: (i, 0)),
                  pl.BlockSpec((WINDOW, 2 * A_KVW), lambda i: (jnp.maximum(i - 1, 0), 0)),
                  pl.BlockSpec((WINDOW, 2 * A_KVW), lambda i: (i, 0))],
        out_specs=pl.BlockSpec((WINDOW, A_QW), lambda i: (i, 0)),
        out_shape=jax.ShapeDtypeStruct((t, A_QW), BF16),
        compiler_params=_params("arbitrary"),
        name="swa_prompt",
    )(sinks, q, kv, kv)


def _swa_sample_kernel(sink_ref, q_ref, kvn_ref, buf_ref, o_ref, nbuf_ref, *, bb, l):
    w = WINDOW
    row = lax.broadcasted_iota(jnp.int32, (l, 2 * w), 0)
    col = lax.broadcasted_iota(jnp.int32, (l, 2 * w), 1)
    dist = row + w - col
    valid = (dist >= 0) & (dist < w)
    distf = dist.astype(F32)
    for b in range(bb):
        buf = buf_ref[b]
        kvn = kvn_ref[b]
        kv = jnp.concatenate([buf, kvn, jnp.zeros((w - l, 2 * A_KVW), F32)], axis=0)
        o = _swa_heads(q_ref[b], kv[:, :A_KVW], kv[:, A_KVW:], valid, distf, sink_ref)
        o_ref[b] = o.astype(BF16)
        nbuf_ref[b, 0:w - l, :] = buf[l:, :]
        nbuf_ref[b, w - l:w, :] = kvn


def _swa_sample(q, kvn, buf, sinks):
    b, l, _ = q.shape
    bb = _tile(b, 8)
    return pl.pallas_call(
        functools.partial(_swa_sample_kernel, bb=bb, l=l),
        grid=(b // bb,),
        in_specs=[pl.BlockSpec(memory_space=pltpu.SMEM),
                  pl.BlockSpec((bb, l, A_QW), lambda i: (i, 0, 0)),
                  pl.BlockSpec((bb, l, 2 * A_KVW), lambda i: (i, 0, 0)),
                  pl.BlockSpec((bb, WINDOW, 2 * A_KVW), lambda i: (i, 0, 0))],
        out_specs=[pl.BlockSpec((bb, l, A_QW), lambda i: (i, 0, 0)),
                   pl.BlockSpec((bb, WINDOW, 2 * A_KVW), lambda i: (i, 0, 0))],
        out_shape=[jax.ShapeDtypeStruct((b, l, A_QW), BF16),
                   jax.ShapeDtypeStruct((b, WINDOW, 2 * A_KVW), F32)],
        compiler_params=_params("arbitrary"),
        name="swa_sample",
    )(sinks, q, kvn, buf)


def _ssd_chunk(conv, z, dt, state, a_row, dsk, gn):
    L = SSD_CHUNK
    gw = D_INNER // B_GROUPS
    hpg = B_HEADS // B_GROUPS
    xs = conv[:, :D_INNER]
    adt = a_row * dt
    r2 = lax.broadcasted_iota(jnp.int32, (L, L), 0)
    c2 = lax.broadcasted_iota(jnp.int32, (L, L), 1)
    lower = r2 >= c2
    acs = jnp.dot(lower.astype(F32), adt, precision=_HI, preferred_element_type=F32)
    acs_t = acs.T
    band = lax.broadcasted_iota(jnp.int32, (L, gw), 1) // B_HEAD_DIM

    ys, states = [], []
    for g in range(B_GROUPS):
        b_g = conv[:, D_INNER + D_STATE * g:D_INNER + D_STATE * (g + 1)].astype(BF16)
        c_g = conv[:, D_INNER + B_GROUPS * D_STATE + D_STATE * g:
                   D_INNER + B_GROUPS * D_STATE + D_STATE * (g + 1)].astype(BF16)
        xs_g = xs[:, gw * g:gw * (g + 1)]

        def expand(mat, g=g):
            out = jnp.broadcast_to(mat[:, hpg * g:hpg * g + 1], (L, gw))
            for r in range(1, hpg):
                out = jnp.where(band == r, jnp.broadcast_to(mat[:, hpg * g + r:hpg * g + r + 1], (L, gw)), out)
            return out

        acs_e = expand(acs)
        xd = xs_g * expand(dt)
        xd_b = xd.astype(BF16)
        gmat = lax.dot_general(c_g, b_g, _NT, preferred_element_type=F32)
        y_diag = None
        for r in range(hpg):
            h = hpg * g + r
            diff = acs[:, h:h + 1] - acs_t[h:h + 1, :]
            lmat = jnp.exp(jnp.where(lower, diff, -jnp.inf))
            yr = jnp.dot((gmat * lmat).astype(BF16), xd_b, preferred_element_type=F32)
            y_diag = yr if y_diag is None else jnp.where(band == r, yr, y_diag)
        s_g = state[gw * g:gw * (g + 1), :]
        y_off = lax.dot_general(c_g, s_g.astype(BF16), _NT, preferred_element_type=F32) * jnp.exp(acs_e)
        decay = jnp.exp(acs_e[L - 1:L, :] - acs_e)
        upd = lax.dot_general((xd * decay).astype(BF16), b_g, _TN, preferred_element_type=F32)
        dec_rows = jnp.concatenate(
            [jnp.broadcast_to(jnp.exp(acs_t[hpg * g + r:hpg * g + r + 1, L - 1:L]), (B_HEAD_DIM, D_STATE))
             for r in range(hpg)], axis=0)
        states.append(s_g * dec_rows + upd)
        ys.append(y_diag + y_off + dsk[:, gw * g:gw * (g + 1)] * xs_g)

    outs = []
    for g in range(B_GROUPS):
        gt = ys[g] * _silu(z[:, gw * g:gw * (g + 1)])
        outs.append(_rms(gt))
    y = jnp.concatenate(outs, axis=-1) * gn
    return y, jnp.concatenate(states, axis=0)


def _conv_from_pad(xp_ref, cw_ref, cb_ref):
    L = SSD_CHUNK
    acc = cb_ref[...] + cw_ref[D_CONV - 1:D_CONV, :] * xp_ref[8:8 + L, :]
    for j in range(D_CONV - 1):
        acc = acc + cw_ref[j:j + 1, :] * xp_ref[5 + j:5 + j + L, :]
    return _silu(acc)


def _ssd_prompt_kernel(xbc_ref, z_ref, dt_ref, cw_ref, cb_ref, dtb_ref, a_ref, dsk_ref, gn_ref,
                       y_ref, st_ref, xp_ref, s_ref):
    i = pl.program_id(0)
    L = SSD_CHUNK

    @pl.when(i == 0)
    def _():
        xp_ref[0:8, :] = jnp.zeros((8, XBC_DIM), F32)
        s_ref[...] = jnp.zeros_like(s_ref)

    xp_ref[8:8 + L, :] = xbc_ref[...]
    conv = _conv_from_pad(xp_ref, cw_ref, cb_ref)
    xp_ref[0:8, :] = xp_ref[L:L + 8, :]
    dt = _softplus(dt_ref[...] + dtb_ref[...])
    y, new_state = _ssd_chunk(conv, z_ref[...], dt, s_ref[...], a_ref[...], dsk_ref[...], gn_ref[...])
    s_ref[...] = new_state
    y_ref[...] = y.astype(BF16)

    @pl.when(i == pl.num_programs(0) - 1)
    def _():
        st_ref[...] = new_state


def _ssd_prompt(xbc, z, dt, cw, cb, dtb, a_row, dsk, gn):
    t = xbc.shape[0]
    L = SSD_CHUNK
    consts = [cw, cb, dtb, a_row, dsk, gn]
    return pl.pallas_call(
        _ssd_prompt_kernel,
        grid=(t // L,),
        in_specs=[pl.BlockSpec((L, XBC_DIM), lambda i: (i, 0)),
                  pl.BlockSpec((L, D_INNER), lambda i: (i, 0)),
                  pl.BlockSpec((L, LANE), lambda i: (i, 0))] + [_const_spec(c) for c in consts],
        out_specs=[pl.BlockSpec((L, D_INNER), lambda i: (i, 0)),
                   pl.BlockSpec((B_HEADS * B_HEAD_DIM, D_STATE), lambda i: (0, 0))],
        out_shape=[jax.ShapeDtypeStruct((t, D_INNER), BF16),
                   jax.ShapeDtypeStruct((B_HEADS * B_HEAD_DIM, D_STATE), F32)],
        scratch_shapes=[pltpu.VMEM((L + 8, XBC_DIM), F32), pltpu.VMEM((B_HEADS * B_HEAD_DIM, D_STATE), F32)],
        compiler_params=_params("arbitrary"),
        name="ssd_prompt",
    )(xbc, z, dt, *consts)


def _ssd_sample_kernel(xbc_ref, z_ref, dt_ref, cs_ref, s0_ref, cw_ref, cb_ref, dtb_ref, a_ref, dsk_ref, gn_ref,
                       y_ref, st_ref, xp_ref, *, l):
    i = pl.program_id(0)
    L = SSD_CHUNK

    @pl.when(i == 0)
    def _():
        xp_ref[...] = jnp.zeros_like(xp_ref)

    xp_ref[5:8, :] = cs_ref[0]
    xp_ref[8:8 + l, :] = xbc_ref[0]
    conv = _conv_from_pad(xp_ref, cw_ref, cb_ref)
    pad = jnp.zeros((L - l, LANE), F32)
    dt = jnp.concatenate([_softplus(dt_ref[0] + dtb_ref[...]), pad], axis=0)
    z = jnp.concatenate([z_ref[0], jnp.zeros((L - l, D_INNER), F32)], axis=0)
    y, new_state = _ssd_chunk(conv, z, dt, s0_ref[0], a_ref[...], dsk_ref[...], gn_ref[...])
    y_ref[0] = y[0:l, :].astype(BF16)
    st_ref[0] = new_state


def _ssd_sample(xbc, z, dt, conv_state, ssm_state, cw, cb, dtb, a_row, dsk, gn):
    b, l, _ = xbc.shape
    L = SSD_CHUNK
    consts = [cw, cb, dtb, a_row, dsk, gn]
    hp = B_HEADS * B_HEAD_DIM
    return pl.pallas_call(
        functools.partial(_ssd_sample_kernel, l=l),
        grid=(b,),
        in_specs=[pl.BlockSpec((1, l, XBC_DIM), lambda i: (i, 0, 0)),
                  pl.BlockSpec((1, l, D_INNER), lambda i: (i, 0, 0)),
                  pl.BlockSpec((1, l, LANE), lambda i: (i, 0, 0)),
                  pl.BlockSpec((1, D_CONV - 1, XBC_DIM), lambda i: (i, 0, 0)),
                  pl.BlockSpec((1, hp, D_STATE), lambda i: (i, 0, 0))] + [_const_spec(c) for c in consts],
        out_specs=[pl.BlockSpec((1, l, D_INNER), lambda i: (i, 0, 0)),
                   pl.BlockSpec((1, hp, D_STATE), lambda i: (i, 0, 0))],
        out_shape=[jax.ShapeDtypeStruct((b, l, D_INNER), BF16),
                   jax.ShapeDtypeStruct((b, hp, D_STATE), F32)],
        scratch_shapes=[pltpu.VMEM((L + 8, XBC_DIM), F32)],
        compiler_params=_params("arbitrary"),
        name="ssd_sample",
    )(xbc, z, dt, conv_state, ssm_state, *consts)


def _proj_res_kernel(*refs, n_in):
    a_refs = refs[:n_in]
    x_ref, g_ref = refs[n_in:n_in + 2]
    w_refs = refs[n_in + 2:2 * n_in + 2]
    o_ref = refs[-1]
    acc = None
    for a_ref, w_ref in zip(a_refs, w_refs):
        part = jnp.dot(a_ref[...], w_ref[...], preferred_element_type=F32)
        acc = part if acc is None else acc + part
    o_ref[...] = x_ref[...] + g_ref[...] * acc


def _proj_res(acts, x, gate, ws):
    t, d = x.shape
    tm = _tile(t, 512)
    n_in = len(acts)
    return pl.pallas_call(
        functools.partial(_proj_res_kernel, n_in=n_in),
        grid=(t // tm,),
        in_specs=[pl.BlockSpec((tm, a.shape[1]), lambda i: (i, 0)) for a in acts]
        + [pl.BlockSpec((tm, d), lambda i: (i, 0)), _row_spec(gate, tm)] + [_const_spec(w) for w in ws],
        out_specs=pl.BlockSpec((tm, d), lambda i: (i, 0)),
        out_shape=jax.ShapeDtypeStruct((t, d), F32),
        compiler_params=_params("arbitrary"),
        name="proj_residual",
    )(*acts, x, gate, *ws)


def _swiglu_kernel(x_ref, sh_ref, sc_ref, g_ref, wgu_ref, wd_ref, o_ref, *, n_chunk):
    x = x_ref[...]
    h = (_rms(x) * (1.0 + sc_ref[...]) + sh_ref[...]).astype(BF16)
    f = wd_ref.shape[0]
    tf = f // n_chunk
    acc = None
    for c in range(n_chunk):
        gp = jnp.dot(h, wgu_ref[:, c * tf:(c + 1) * tf], preferred_element_type=F32)
        up = jnp.dot(h, wgu_ref[:, f + c * tf:f + (c + 1) * tf], preferred_element_type=F32)
        a = (_silu(gp) * up).astype(BF16)
        part = jnp.dot(a, wd_ref[c * tf:(c + 1) * tf, :], preferred_element_type=F32)
        acc = part if acc is None else acc + part
    o_ref[...] = x + g_ref[...] * acc


def _swiglu(x, sh, sc, gate, wgu, wd):
    t, d = x.shape
    tm = _tile(t, 512)
    return pl.pallas_call(
        functools.partial(_swiglu_kernel, n_chunk=2),
        grid=(t // tm,),
        in_specs=[pl.BlockSpec((tm, d), lambda i: (i, 0)), _row_spec(sh, tm), _row_spec(sc, tm), _row_spec(gate, tm),
                  _const_spec(wgu), _const_spec(wd)],
        out_specs=pl.BlockSpec((tm, d), lambda i: (i, 0)),
        out_shape=jax.ShapeDtypeStruct((t, d), F32),
        compiler_params=_params("arbitrary"),
        name="swiglu",
    )(x, sh, sc, gate, wgu, wd)


def _rope_lanes(blk, t_ref):
    half = C_ROPE // 2
    return (blk * t_ref[0] + pltpu.roll(blk, half, 1) * t_ref[1]
            + pltpu.roll(blk, LANE - half, 1) * t_ref[2])


def _od_latents(x_ref, sh_ref, sc_ref, w_in_ref, qn_ref, kvn_ref):
    x = x_ref[...]
    h = (_rms(x) * (1.0 + sc_ref[...]) + sh_ref[...]).astype(BF16)
    y = jnp.dot(h, w_in_ref[...], preferred_element_type=F32)
    cqn = (_rms(y[:, :Q_LORA]) * qn_ref[...]).astype(BF16)
    ckvn = _rms(y[:, Q_LORA:Q_LORA + KV_LORA]) * kvn_ref[...]
    return cqn, ckvn, y[:, Q_LORA + KV_LORA:]


def _od_in_prompt_kernel(x_ref, sh_ref, sc_ref, w_in_ref, qn_ref, kvn_ref, wuq_ref, wuk_ref, wuv_ref,
                         tq_ref, tk_ref, q_out, k_out, v_out, ckv_out, kpe_out):
    cqn, ckvn, kpe_pad = _od_latents(x_ref, sh_ref, sc_ref, w_in_ref, qn_ref, kvn_ref)
    ckv_out[...] = ckvn
    ckb = ckvn.astype(BF16)
    qf = jnp.dot(cqn, wuq_ref[...], preferred_element_type=F32)
    kf = jnp.dot(ckb, wuk_ref[...], preferred_element_type=F32)
    v_out[0] = lax.dot_general(wuv_ref[...], ckb, _NT, preferred_element_type=F32).astype(BF16)
    kpr = _rope_lanes(kpe_pad, tk_ref)
    kpe_out[...] = kpr[:, C_NOPE:C_NOPE + C_ROPE]
    for hh in range(C_HEADS):
        q_out[hh] = _rope_lanes(qf[:, LANE * hh:LANE * (hh + 1)], tq_ref).astype(BF16)
        k_out[hh] = (kf[:, LANE * hh:LANE * (hh + 1)] + kpr).astype(BF16)


def _od_in_prompt(x, sh, sc, w_in, qn, kvn, wuq, wuk, wuv, tq, tk):
    t, d = x.shape
    tm = _tile(t, MLA_TK)
    consts = [w_in, qn, kvn, wuq, wuk, wuv]
    return pl.pallas_call(
        _od_in_prompt_kernel,
        grid=(t // tm,),
        in_specs=[pl.BlockSpec((tm, d), lambda i: (i, 0)), _row_spec(sh, tm), _row_spec(sc, tm)]
        + [_const_spec(c) for c in consts]
        + [pl.BlockSpec((3, tm, LANE), lambda i: (0, i, 0)), pl.BlockSpec((3, tm, LANE), lambda i: (0, i, 0))],
        out_specs=[pl.BlockSpec((C_HEADS, tm, LANE), lambda i: (0, i, 0)),
                   pl.BlockSpec((C_HEADS, tm, LANE), lambda i: (0, i, 0)),
                   pl.BlockSpec((1, C_HEADS * C_V, tm), lambda i: (i, 0, 0)),
                   pl.BlockSpec((tm, KV_LORA), lambda i: (i, 0)),
                   pl.BlockSpec((tm, C_ROPE), lambda i: (i, 0))],
        out_shape=[jax.ShapeDtypeStruct((C_HEADS, t, LANE), BF16),
                   jax.ShapeDtypeStruct((C_HEADS, t, LANE), BF16),
                   jax.ShapeDtypeStruct((t // tm, C_HEADS * C_V, tm), BF16),
                   jax.ShapeDtypeStruct((t, KV_LORA), F32),
                   jax.ShapeDtypeStruct((t, C_ROPE), F32)],
        compiler_params=_params("arbitrary"),
        name="mla_proj_prompt",
    )(x, sh, sc, *consts, tq, tk)


def _od_in_sample_kernel(x_ref, sh_ref, sc_ref, w_in_ref, qn_ref, kvn_ref, wuq_ref, wukt_ref,
                         tq_ref, tk_ref, ql_out, qp_out, ckv_out, kpe_out, kpp_out):
    cqn, ckvn, kpe_pad = _od_latents(x_ref, sh_ref, sc_ref, w_in_ref, qn_ref, kvn_ref)
    ckv_out[...] = ckvn
    kpr = _rope_lanes(kpe_pad, tk_ref)
    kpe_out[...] = kpr[:, 0:C_ROPE]
    kpp_out[...] = kpr
    qf = jnp.dot(cqn, wuq_ref[...], preferred_element_type=F32)
    nq = C_HEADS * LANE
    for hh in range(C_HEADS):
        qn_h = (qf[:, LANE * hh:LANE * (hh + 1)] * MLA_SCALE).astype(BF16)
        ql = jnp.dot(qn_h, wukt_ref[hh], preferred_element_type=F32)
        ql_out[:, KV_LORA * hh:KV_LORA * (hh + 1)] = ql.astype(BF16)
        qp_out[:, LANE * hh:LANE * (hh + 1)] = _rope_lanes(qf[:, nq + LANE * hh:nq + LANE * (hh + 1)], tq_ref).astype(BF16)


def _od_in_sample(x, sh, sc, w_in, qn, kvn, wuq, wukt, tq, tk):
    t, d = x.shape
    tm = _tile(t, 512)
    consts = [w_in, qn, kvn, wuq, wukt]
    widths = [(C_HEADS * KV_LORA, BF16), (C_HEADS * LANE, BF16), (KV_LORA, F32), (C_ROPE, F32), (LANE, F32)]
    return pl.pallas_call(
        _od_in_sample_kernel,
        grid=(t // tm,),
        in_specs=[pl.BlockSpec((tm, d), lambda i: (i, 0)), _row_spec(sh, tm), _row_spec(sc, tm)]
        + [_const_spec(c) for c in consts]
        + [pl.BlockSpec((3, tm, LANE), lambda i: (0, i, 0)), pl.BlockSpec((3, tm, LANE), lambda i: (0, i, 0))],
        out_specs=[pl.BlockSpec((tm, n), lambda i: (i, 0)) for n, _ in widths],
        out_shape=[jax.ShapeDtypeStruct((t, n), dt) for n, dt in widths],
        compiler_params=_params("arbitrary"),
        name="mla_proj_sample",
    )(x, sh, sc, *consts, tq, tk)


def _flash_kernel(q_ref, k_ref, vt_ref, o_ref, m_ref, l_ref, acc_ref, sa_ref, mxa_ref, sb_ref, mxb_ref, *, tq, tk):
    qi = pl.program_id(1)
    m_ref[...] = jnp.full(m_ref.shape, -jnp.inf, F32)
    l_ref[...] = jnp.zeros(l_ref.shape, F32)
    acc_ref[...] = jnp.zeros(acc_ref.shape, F32)

    def scores(j, a, masked):
        start = pl.multiple_of(j * tk, tk)
        st = lax.dot_general(k_ref[a, pl.ds(start, tk), :], q_ref[a], _NT, preferred_element_type=F32)
        if masked:
            krow = lax.broadcasted_iota(jnp.int32, (tk, tq), 0) + (j * tk - qi * tq)
            qcol = lax.broadcasted_iota(jnp.int32, (tk, tq), 1)
            st = jnp.where(krow <= qcol, st, -jnp.inf)
        return st

    def absorb(j, a, st, mx):
        m_prev = m_ref[a]
        m_new = jnp.maximum(m_prev, mx)
        alpha = jnp.exp2(m_prev - m_new)
        p = jnp.exp2(st - m_new)
        vt1 = jnp.concatenate([vt_ref[j, C_V * a:C_V * (a + 1), :], jnp.ones((16, tk), BF16)], axis=0)
        pv = jnp.dot(vt1, p.astype(BF16), preferred_element_type=F32)
        l_ref[a] = alpha * l_ref[a] + pv[C_V:C_V + 1, :]
        acc_ref[a] = alpha * acc_ref[a] + pv[:C_V, :]
        m_ref[a] = m_new

    n_full = (qi * tq) // tk
    for dd in range(max(tq // tk, 1)):
        for a in range(2):
            st = scores(n_full + dd, a, True)
            absorb(n_full + dd, a, st, jnp.max(st, axis=0, keepdims=True))

    def produce(j, buf):
        s_ref, mx_ref = buf
        for a in range(2):
            st = scores(j, a, False)
            s_ref[a] = st
            mx_ref[a] = jnp.max(st, axis=0, keepdims=True)

    def consume(j, buf):
        s_ref, mx_ref = buf
        for a in range(2):
            absorb(j, a, s_ref[a], mx_ref[a])

    buf_a, buf_b = (sa_ref, mxa_ref), (sb_ref, mxb_ref)

    @pl.when(n_full > 0)
    def _():
        produce(0, buf_a)
        n_loop = (n_full - 1) // 2

        def body(k, carry):
            produce(2 * k + 1, buf_b)
            consume(2 * k, buf_a)
            produce(2 * k + 2, buf_a)
            consume(2 * k + 1, buf_b)
            return carry

        lax.fori_loop(0, n_loop, body, 0)
        j0 = 2 * n_loop

        @pl.when(n_full - j0 == 2)
        def _():
            produce(j0 + 1, buf_b)
            consume(j0, buf_a)
            consume(j0 + 1, buf_b)

        @pl.when(n_full - j0 == 1)
        def _():
            consume(j0, buf_a)
    o_t = jnp.concatenate([acc_ref[0] / l_ref[0], acc_ref[1] / l_ref[1]], axis=0)
    o_ref[...] = o_t.T.astype(BF16)


def _flash_prompt(qh, kh, vt):
    nk, _, tk = vt.shape
    t = nk * tk
    tq = _tile(t, MLA_TQ)
    return pl.pallas_call(
        functools.partial(_flash_kernel, tq=tq, tk=tk),
        grid=(C_HEADS // 2, t // tq),
        in_specs=[pl.BlockSpec((2, tq, LANE), lambda hp, qi: (hp, qi, 0)),
                  pl.BlockSpec((2, t, LANE), lambda hp, qi: (hp, 0, 0)),
                  pl.BlockSpec((nk, 2 * C_V, tk), lambda hp, qi: (0, hp, 0))],
        out_specs=pl.BlockSpec((tq, LANE), lambda hp, qi: (qi, hp)),
        out_shape=jax.ShapeDtypeStruct((t, C_HEADS * C_V), BF16),
        scratch_shapes=[pltpu.VMEM((2, 1, tq), F32), pltpu.VMEM((2, 1, tq), F32), pltpu.VMEM((2, C_V, tq), F32),
                        pltpu.VMEM((2, tk, tq), F32), pltpu.VMEM((2, 1, tq), F32),
                        pltpu.VMEM((2, tk, tq), F32), pltpu.VMEM((2, 1, tq), F32)],
        compiler_params=_params("arbitrary", "arbitrary"),
        name="mla_flash_prompt",
    )(qh, kh, vt)


def _paged_kernel(pt_ref, ql_ref, qp_ref, cn_ref, kn_ref, ckv_hbm, kpt_hbm, o_ref,
                  ck_buf, kp_buf, ck_sem, kp_sem, ckb_ref, s_ref, mx_ref, m_ref, l_ref, acc_ref, *, ch, l):
    b = pl.program_id(0)
    nb = pl.num_programs(0)
    n_chunks = pt_ref.shape[1] // ch
    nq = ql_ref.shape[1]

    def ck_copy(bi, c, p, slot):
        page = pt_ref[bi, c * ch + p]
        return pltpu.make_async_copy(ckv_hbm.at[0, page], ck_buf.at[slot, pl.ds(p * PAGE_SIZE, PAGE_SIZE), :],
                                     ck_sem.at[slot])

    def kp_copy(bi, c, p, slot):
        page = pt_ref[bi, c * ch + p]
        return pltpu.make_async_copy(kpt_hbm.at[0, page], kp_buf.at[slot, 0:C_ROPE, pl.ds(p * PAGE_SIZE, PAGE_SIZE)],
                                     kp_sem.at[slot])

    def start_chunk(bi, c, slot):
        for p in range(ch):
            ck_copy(bi, c, p, slot).start()
            kp_copy(bi, c, p, slot).start()

    def wait_chunk(slot):
        for p in range(ch):
            ck_copy(0, 0, p, slot).wait()
            kp_copy(0, 0, p, slot).wait()

    @pl.when(b == 0)
    def _():
        kp_buf[...] = jnp.zeros_like(kp_buf)
        start_chunk(0, 0, 0)
        start_chunk(0, 1, 1)

    m_ref[...] = jnp.full(m_ref.shape, -jnp.inf, F32)
    l_ref[...] = jnp.zeros(l_ref.shape, F32)
    acc_ref[...] = jnp.zeros(acc_ref.shape, F32)
    ql = ql_ref[0]
    qp = qp_ref[0]

    def absorb(s, mx, values):
        m_prev = m_ref[...]
        m_new = jnp.maximum(m_prev, mx)
        alpha = jnp.exp(m_prev - m_new)
        p = jnp.exp(s - m_new)
        l_ref[...] = alpha * l_ref[...] + jnp.sum(p, axis=-1, keepdims=True)
        acc_ref[...] = alpha * acc_ref[...] + jnp.dot(p.astype(BF16), values, preferred_element_type=F32)
        m_ref[...] = m_new

    def produce(slot):
        ck = ck_buf[slot].astype(BF16)
        kp = kp_buf[slot].astype(BF16)
        s = lax.dot_general(ql, ck, _NT, preferred_element_type=F32) + jnp.dot(qp, kp, preferred_element_type=F32)
        ckb_ref[slot] = ck
        s_ref[slot] = s
        mx_ref[slot] = jnp.max(s, axis=-1, keepdims=True)

    def consume(slot):
        absorb(s_ref[slot], mx_ref[slot], ckb_ref[slot])

    wait_chunk(0)
    produce(0)

    def pair(k, carry):
        start_chunk(b, 2 * k + 2, 0)
        wait_chunk(1)
        produce(1)
        consume(0)
        start_chunk(b, 2 * k + 3, 1)
        wait_chunk(0)
        produce(0)
        consume(1)
        return carry

    lax.fori_loop(0, n_chunks // 2 - 1, pair, 0)

    @pl.when(b + 1 < nb)
    def _():
        start_chunk(b + 1, 0, 0)

    wait_chunk(1)
    produce(1)
    consume(0)

    @pl.when(b + 1 < nb)
    def _():
        start_chunk(b + 1, 1, 1)

    consume(1)

    cn = jnp.concatenate([cn_ref[0], jnp.zeros((LANE - l, KV_LORA), F32)], axis=0).astype(BF16)
    kn_t = jnp.concatenate([kn_ref[0], jnp.zeros((LANE - l, LANE), F32)], axis=0).T.astype(BF16)
    s = lax.dot_general(ql, cn, _NT, preferred_element_type=F32) + jnp.dot(qp, kn_t, preferred_element_type=F32)
    qtok = lax.broadcasted_iota(jnp.int32, (nq, LANE), 0) // C_HEADS
    kcol = lax.broadcasted_iota(jnp.int32, (nq, LANE), 1)
    s = jnp.where(kcol <= qtok, s, -jnp.inf)
    absorb(s, jnp.max(s, axis=-1, keepdims=True), cn)
    o_ref[0] = (acc_ref[...] / l_ref[...]).astype(BF16)


def _paged_attention(page_table, ql, qp, ckv_new, kpe_new_pad, cache_ckv, cache_kpe_t):
    b, nq, _ = ql.shape
    l = ckv_new.shape[1]
    n_pages = page_table.shape[1]
    ch = max(c for c in (16, 8, 4, 2, 1) if n_pages % (2 * c) == 0)
    grid_spec = pltpu.PrefetchScalarGridSpec(
        num_scalar_prefetch=1,
        grid=(b,),
        in_specs=[pl.BlockSpec((1, nq, KV_LORA), lambda bi, pt: (bi, 0, 0)),
                  pl.BlockSpec((1, nq, LANE), lambda bi, pt: (bi, 0, 0)),
                  pl.BlockSpec((1, l, KV_LORA), lambda bi, pt: (bi, 0, 0)),
                  pl.BlockSpec((1, l, LANE), lambda bi, pt: (bi, 0, 0)),
                  pl.BlockSpec(memory_space=pl.ANY),
                  pl.BlockSpec(memory_space=pl.ANY)],
        out_specs=pl.BlockSpec((1, nq, KV_LORA), lambda bi, pt: (bi, 0, 0)),
        scratch_shapes=[pltpu.VMEM((2, ch * PAGE_SIZE, KV_LORA), F32),
                        pltpu.VMEM((2, LANE, ch * PAGE_SIZE), F32),
                        pltpu.SemaphoreType.DMA((2,)),
                        pltpu.SemaphoreType.DMA((2,)),
                        pltpu.VMEM((2, ch * PAGE_SIZE, KV_LORA), BF16),
                        pltpu.VMEM((2, nq, ch * PAGE_SIZE), F32),
                        pltpu.VMEM((2, nq, 1), F32),
                        pltpu.VMEM((nq, 1), F32), pltpu.VMEM((nq, 1), F32), pltpu.VMEM((nq, KV_LORA), F32)],
    )
    return pl.pallas_call(
        functools.partial(_paged_kernel, ch=ch, l=l),
        grid_spec=grid_spec,
        out_shape=jax.ShapeDtypeStruct((b, nq, KV_LORA), BF16),
        compiler_params=_params("arbitrary"),
        name="mla_paged_sample",
    )(page_table, ql, qp, ckv_new, kpe_new_pad, cache_ckv, cache_kpe_t)


def _mla_out_sample_kernel(o_ref, x_ref, g_ref, wuv_ref, wo_ref, out_ref):
    parts = []
    for pr in range(C_HEADS // 2):
        parts.append(jnp.dot(o_ref[:, 2 * KV_LORA * pr:2 * KV_LORA * (pr + 1)], wuv_ref[pr],
                             preferred_element_type=F32))
    attn = jnp.concatenate(parts, axis=-1).astype(BF16)
    out_ref[...] = x_ref[...] + g_ref[...] * jnp.dot(attn, wo_ref[...], preferred_element_type=F32)


def _mla_out_sample(o_lat, x, gate, wuv_pairs, wo):
    t, d = x.shape
    tm = _tile(t, 512)
    return pl.pallas_call(
        _mla_out_sample_kernel,
        grid=(t // tm,),
        in_specs=[pl.BlockSpec((tm, o_lat.shape[1]), lambda i: (i, 0)), pl.BlockSpec((tm, d), lambda i: (i, 0)),
                  _row_spec(gate, tm), _const_spec(wuv_pairs), _const_spec(wo)],
        out_specs=pl.BlockSpec((tm, d), lambda i: (i, 0)),
        out_shape=jax.ShapeDtypeStruct((t, d), F32),
        compiler_params=_params("arbitrary"),
        name="mla_out_sample",
    )(o_lat, x, gate, wuv_pairs, wo)


def _route_kernel(x_ref, sh_ref, sc_ref, r_ref, h_ref, gate_ref, idx_ref):
    x = x_ref[...]
    h = _rms(x) * (1.0 + sc_ref[...]) + sh_ref[...]
    h_ref[...] = h
    logits = jnp.dot(h, r_ref[...], precision=_HI, preferred_element_type=F32)
    lane = lax.broadcasted_iota(jnp.int32, logits.shape, 1)
    logits = jnp.where(lane < N_EXPERTS, logits, -jnp.inf)
    v1 = jnp.max(logits, axis=-1, keepdims=True)
    i1 = jnp.min(jnp.where(logits == v1, lane, LANE), axis=-1, keepdims=True)
    rest = jnp.where(lane == i1, -jnp.inf, logits)
    v2 = jnp.max(rest, axis=-1, keepdims=True)
    i2 = jnp.min(jnp.where(rest == v2, lane, LANE), axis=-1, keepdims=True)
    e2 = jnp.exp(v2 - v1)
    gate_ref[...] = jnp.where(lane == 0, 1.0 / (1.0 + e2), jnp.where(lane == 1, e2 / (1.0 + e2), 0.0))
    idx_ref[...] = jnp.where(lane == 0, i1, jnp.where(lane == 1, i2, 0))


def _route(x, sh, sc, router_pad):
    t, d = x.shape
    tm = _tile(t, 512)
    return pl.pallas_call(
        _route_kernel,
        grid=(t // tm,),
        in_specs=[pl.BlockSpec((tm, d), lambda i: (i, 0)), _row_spec(sh, tm), _row_spec(sc, tm), _const_spec(router_pad)],
        out_specs=[pl.BlockSpec((tm, d), lambda i: (i, 0)), pl.BlockSpec((tm, LANE), lambda i: (i, 0)),
                   pl.BlockSpec((tm, LANE), lambda i: (i, 0))],
        out_shape=[jax.ShapeDtypeStruct((t, d), F32), jax.ShapeDtypeStruct((t, LANE), F32),
                   jax.ShapeDtypeStruct((t, LANE), jnp.int32)],
        compiler_params=_params("arbitrary"),
        name="moe_route",
    )(x, sh, sc, router_pad)


def _moe_plan(idx, tm):
    t = idx.shape[0]
    n_tiles = -(-2 * t // tm) + N_EXPERTS
    e_flat = idx.reshape(-1)
    onehot = (e_flat[:, None] == jnp.arange(N_EXPERTS, dtype=jnp.int32)[None]).astype(jnp.int32)
    csum = jnp.cumsum(onehot, axis=0)
    rank = jnp.sum(csum * onehot, axis=1) - 1
    padded = (csum[-1] + tm - 1) // tm * tm
    gend = jnp.cumsum(padded)
    dest = (gend - padded)[e_flat] + rank
    row_token = jnp.zeros((n_tiles * tm,), jnp.int32).at[dest].set(jnp.arange(2 * t, dtype=jnp.int32) // 2)
    n_used = gend[-1] // tm
    tile_start = jnp.minimum(jnp.arange(n_tiles, dtype=jnp.int32), n_used - 1) * tm
    tile_expert = jnp.sum((gend[None, :] <= tile_start[:, None]).astype(jnp.int32), axis=1)
    return dest, row_token, tile_expert.astype(jnp.int32), n_used.reshape(1).astype(jnp.int32)


def _row_loop(n, fn):
    def body(r, carry):
        fn(r)
        return carry

    lax.fori_loop(0, n, body, 0, unroll=8)


def _moe_gather_kernel(rt_ref, h_hbm, xs_ref, sem, *, tm):
    i = pl.program_id(0)

    def row_copy(r, tok):
        return pltpu.make_async_copy(h_hbm.at[pl.ds(tok, 1)], xs_ref.at[pl.ds(r, 1)], sem.at[0])

    _row_loop(tm, lambda r: row_copy(r, rt_ref[i * tm + r]).start())
    _row_loop(tm, lambda r: row_copy(0, 0).wait())


def _moe_gather(row_token, h, tm):
    r = row_token.shape[0]
    d = h.shape[1]
    return pl.pallas_call(
        functools.partial(_moe_gather_kernel, tm=tm),
        grid_spec=pltpu.PrefetchScalarGridSpec(
            num_scalar_prefetch=1, grid=(r // tm,),
            in_specs=[pl.BlockSpec(memory_space=pl.ANY)],
            out_specs=pl.BlockSpec((tm, d), lambda i, rt: (i, 0)),
            scratch_shapes=[pltpu.SemaphoreType.DMA((1,))]),
        out_shape=jax.ShapeDtypeStruct((r, d), h.dtype),
        compiler_params=_params("arbitrary"),
        name="moe_gather",
    )(row_token, h)


def _moe_grouped_kernel(te_ref, nu_ref, xs_ref, wg_ref, wu_ref, wd_ref, y_ref, xb_ref):
    del te_ref
    i = pl.program_id(0)
    c = pl.program_id(1)

    @pl.when(c == 0)
    def _():
        xb_ref[...] = xs_ref[...].astype(BF16)
        y_ref[...] = jnp.zeros_like(y_ref)

    @pl.when(i < nu_ref[0])
    def _():
        xb = xb_ref[...]
        gp = jnp.dot(xb, wg_ref[0], preferred_element_type=F32)
        up = jnp.dot(xb, wu_ref[0], preferred_element_type=F32)
        y_ref[...] += jnp.dot((_silu(gp) * up).astype(BF16), wd_ref[0], preferred_element_type=F32)


def _moe_grouped(tile_expert, n_used, xs, wgu, wd, tm):
    r, d = xs.shape
    f = wd.shape[1]
    tf = max(c for c in (896, 512, 256, 128) if f % c == 0)
    nc = f // tf

    def chunk(i, c, nu):
        return jnp.where(i < nu[0], c, nc - 1)

    return pl.pallas_call(
        _moe_grouped_kernel,
        grid_spec=pltpu.PrefetchScalarGridSpec(
            num_scalar_prefetch=2, grid=(r // tm, nc),
            in_specs=[pl.BlockSpec((tm, d), lambda i, c, te, nu: (i, 0)),
                      pl.BlockSpec((1, d, tf), lambda i, c, te, nu: (te[i], 0, chunk(i, c, nu))),
                      pl.BlockSpec((1, d, tf), lambda i, c, te, nu: (te[i], 0, nc + chunk(i, c, nu))),
                      pl.BlockSpec((1, tf, d), lambda i, c, te, nu: (te[i], chunk(i, c, nu), 0))],
            out_specs=pl.BlockSpec((tm, d), lambda i, c, te, nu: (i, 0)),
            scratch_shapes=[pltpu.VMEM((tm, d), BF16)]),
        out_shape=jax.ShapeDtypeStruct((r, d), F32),
        compiler_params=_params("arbitrary", "arbitrary"),
        name="moe_grouped",
    )(tile_expert, n_used, xs, wgu, wgu, wd)


def _moe_combine_kernel(pos_ref, x_ref, g2_ref, gate_ref, fn_ref, y_hbm, o_ref, ya_ref, yb_ref, sem, *, tm):
    i = pl.program_id(0)
    slot = i % 2

    def row_copy(buf, s, r, src):
        return pltpu.make_async_copy(y_hbm.at[pl.ds(src, 1)], buf.at[s, pl.ds(r, 1)], sem.at[s])

    def issue(step, s):
        def one(r):
            t = step * tm + r
            row_copy(ya_ref, s, r, pos_ref[2 * t]).start()
            row_copy(yb_ref, s, r, pos_ref[2 * t + 1]).start()

        _row_loop(tm, one)

    @pl.when(i == 0)
    def _():
        issue(0, 0)

    @pl.when(i + 1 < pl.num_programs(0))
    def _():
        issue(i + 1, 1 - slot)

    def wait_one(r):
        row_copy(ya_ref, slot, 0, 0).wait()
        row_copy(yb_ref, slot, 0, 0).wait()

    _row_loop(tm, wait_one)
    gate = gate_ref[...]
    y = gate[:, 0:1] * ya_ref[slot] + gate[:, 1:2] * yb_ref[slot]
    xo = x_ref[...] + g2_ref[...] * y
    o_ref[...] = _rms(xo) * fn_ref[...]


def _moe_combine(pos, x, g2, gate, fn, y):
    t, d = x.shape
    tm = _tile(t, 256)
    return pl.pallas_call(
        functools.partial(_moe_combine_kernel, tm=tm),
        grid_spec=pltpu.PrefetchScalarGridSpec(
            num_scalar_prefetch=1, grid=(t // tm,),
            in_specs=[pl.BlockSpec((tm, d), lambda i, p: (i, 0)),
                      (pl.BlockSpec((1, d), lambda i, p: (0, 0)) if g2.shape[0] == 1
                       else pl.BlockSpec((tm, d), lambda i, p: (i, 0))),
                      pl.BlockSpec((tm, LANE), lambda i, p: (i, 0)),
                      pl.BlockSpec((1, d), lambda i, p: (0, 0)),
                      pl.BlockSpec(memory_space=pl.ANY)],
            out_specs=pl.BlockSpec((tm, d), lambda i, p: (i, 0)),
            scratch_shapes=[pltpu.VMEM((2, tm, d), F32), pltpu.VMEM((2, tm, d), F32), pltpu.SemaphoreType.DMA((2,))]),
        out_shape=jax.ShapeDtypeStruct((t, d), F32),
        compiler_params=_params("arbitrary"),
        name="moe_combine",
    )(pos, x, g2, gate, fn, y)


def _rope_tables(pos, lo, reps, scale):
    half = C_ROPE // 2
    freqs = ROPE_THETA ** (-2.0 * jnp.arange(half, dtype=F32) / C_ROPE)
    ang = pos.astype(F32)[:, None] * freqs[None]
    cos, sin = jnp.cos(ang), jnp.sin(ang)
    n = pos.shape[0]
    zeros = jnp.zeros((n, half), F32)
    c_grp = jnp.concatenate([cos, cos], axis=1)
    s1_grp = jnp.concatenate([zeros, sin], axis=1)
    s2_grp = jnp.concatenate([-sin, zeros], axis=1)

    def lay(grp, fill):
        body = jnp.tile(grp, (1, reps))
        left = jnp.full((n, lo), fill, F32)
        right = jnp.zeros((n, LANE - lo - reps * C_ROPE), F32)
        return jnp.concatenate([left, body, right], axis=1)

    return jnp.stack([lay(c_grp, 1.0), lay(s1_grp, 0.0), lay(s2_grp, 0.0)]) * scale


def _pad_cols(w, n):
    return jnp.pad(w, ((0, 0), (0, n - w.shape[1])))


def kernel(x_prompt, x_sample, c_prompt, c_sample, state_swa_kv, state_conv, state_ssm, cache_ckv, cache_kpe, page_table,
           ev_mod_w, ev_mod_b, ev_w_in, ev_sinks, ev_conv_w, ev_conv_b, ev_dt_bias, ev_a_log, ev_d_skip, ev_gnorm,
           ev_w_out, ev_w_gu, ev_w_down, od_mod_w, od_mod_b, od_w_in, od_qnorm, od_kvnorm, od_w_uq, od_w_uk, od_w_uv,
           od_w_out, od_router, od_w_gu, od_w_down, final_norm):
    d = D_MODEL
    _, tp, _ = x_prompt.shape
    bs, ls, _ = x_sample.shape
    ts = bs * ls
    n_pages = page_table.shape[1]
    past = n_pages * PAGE_SIZE
    assert state_swa_kv.shape[2] == WINDOW and ls <= 8 and ls >= D_CONV - 1

    xp = x_prompt.reshape(tp, d)
    xs = x_sample.reshape(ts, d)

    n_c = 1 + bs
    n_cp = -(-n_c // 8) * 8
    c_all = jnp.pad(jnp.concatenate([c_prompt, c_sample], axis=0), ((0, n_cp - n_c), (0, 0)))

    def mods(w, b):
        m = _ada_mod(c_all, w, b)
        mp = [m[0:1, k * d:(k + 1) * d] for k in range(MOD_SLOTS)]
        ms = [jnp.repeat(m[1:n_c, k * d:(k + 1) * d], ls, axis=0) for k in range(MOD_SLOTS)]
        return mp, ms

    i = 0
    mp, ms = mods(ev_mod_w[i], ev_mod_b[i])
    w_in = ev_w_in[i]
    o_k, o_v, o_z, o_x, o_dt = A_QW, A_QW + A_KVW, A_QW + 2 * A_KVW, A_QW + 2 * A_KVW + D_INNER, A_QW + 2 * A_KVW + D_INNER + XBC_DIM
    w_in_p = jnp.concatenate([w_in[:, :o_k], w_in[:, o_z:o_x], w_in[:, o_x:o_dt], w_in[:, o_k:o_z],
                              _pad_cols(w_in[:, o_dt:], LANE)], axis=1).astype(BF16)
    sinks = ev_sinks[i].astype(F32)
    cw = ev_conv_w[i]
    cb = ev_conv_b[i].reshape(1, XBC_DIM)
    dtb = _pad_cols(ev_dt_bias[i].reshape(1, B_HEADS).astype(F32), LANE)
    a_row = _pad_cols(-jnp.exp(ev_a_log[i].astype(F32)).reshape(1, B_HEADS), LANE)
    dsk = jnp.repeat(ev_d_skip[i].astype(F32), B_HEAD_DIM).reshape(1, D_INNER)
    gn = ev_gnorm[i].reshape(1, D_INNER)
    w_out = ev_w_out[i].astype(BF16)
    w_gu = ev_w_gu[i].astype(BF16)
    w_dn = ev_w_down[i].astype(BF16)
    ssd_consts = (cw, cb, dtb, a_row, dsk, gn)

    q, z, xbc, kv, dt = _ev_in(xp, mp[0], mp[1], w_in_p)
    attn = _swa_prompt(q, kv, sinks)
    ssm, ssm_state_p = _ssd_prompt(xbc, z, dt, *ssd_consts)
    xp = _proj_res([attn, ssm], xp, mp[2], [w_out[:A_QW], w_out[A_QW:]])
    xp = _swiglu(xp, mp[3], mp[4], mp[5], w_gu, w_dn)
    swa_kv_prompt = kv[tp - WINDOW:].reshape(1, 1, WINDOW, 2, A_KV_HEADS, A_HEAD_DIM)
    conv_prompt = xbc[tp - (D_CONV - 1):].reshape(1, 1, D_CONV - 1, XBC_DIM)
    ssm_prompt = ssm_state_p.reshape(1, 1, B_HEADS, B_HEAD_DIM, D_STATE)

    q, z, xbc, kv, dt = _ev_in(xs, ms[0], ms[1], w_in_p)
    buf = state_swa_kv[i].reshape(bs, WINDOW, 2 * A_KVW)
    attn, nbuf = _swa_sample(q.reshape(bs, ls, A_QW), kv.reshape(bs, ls, 2 * A_KVW), buf, sinks)
    xbc3 = xbc.reshape(bs, ls, XBC_DIM)
    ssm, ssm_state_s = _ssd_sample(xbc3, z.reshape(bs, ls, D_INNER), dt.reshape(bs, ls, LANE), state_conv[i],
                                   state_ssm[i].reshape(bs, B_HEADS * B_HEAD_DIM, D_STATE), *ssd_consts)
    xs = _proj_res([attn.reshape(ts, A_QW), ssm.reshape(ts, D_INNER)], xs, ms[2], [w_out[:A_QW], w_out[A_QW:]])
    xs = _swiglu(xs, ms[3], ms[4], ms[5], w_gu, w_dn)
    swa_kv_sample = nbuf.reshape(1, bs, WINDOW, 2, A_KV_HEADS, A_HEAD_DIM)
    conv_sample = xbc3[:, ls - (D_CONV - 1):].reshape(1, bs, D_CONV - 1, XBC_DIM)
    ssm_sample = ssm_state_s.reshape(1, bs, B_HEADS, B_HEAD_DIM, D_STATE)

    mp, ms = mods(od_mod_w[i], od_mod_b[i])
    w_in = od_w_in[i]
    w_cq, w_ckv, w_kpe = w_in[:, :Q_LORA], w_in[:, Q_LORA:Q_LORA + KV_LORA], w_in[:, Q_LORA + KV_LORA:]
    qn = od_qnorm[i].reshape(1, Q_LORA)
    kvn = od_kvnorm[i].reshape(1, KV_LORA)
    w_uq = od_w_uq[i].reshape(Q_LORA, C_HEADS, C_NOPE + C_ROPE)
    w_uk = od_w_uk[i]
    w_uv = od_w_uv[i]
    w_o = od_w_out[i].astype(BF16)

    w_in_pp = jnp.concatenate([w_cq, w_ckv, jnp.pad(w_kpe, ((0, 0), (C_NOPE, LANE - C_NOPE - C_ROPE)))], axis=1).astype(BF16)
    wuq_p = jnp.pad(w_uq, ((0, 0), (0, 0), (0, LANE - C_NOPE - C_ROPE))).reshape(Q_LORA, C_HEADS * LANE).astype(BF16)
    wuk_p = jnp.pad(w_uk, ((0, 0), (0, 0), (0, LANE - C_NOPE))).reshape(KV_LORA, C_HEADS * LANE).astype(BF16)
    wuv_p = w_uv.reshape(KV_LORA, C_HEADS * C_V).T.astype(BF16)
    pos_p = jnp.arange(tp, dtype=jnp.int32)
    tq_p = _rope_tables(pos_p, C_NOPE, 1, MLA_SCALE * LOG2E)
    tk_p = _rope_tables(pos_p, C_NOPE, 1, 1.0)
    qh, kh, v, ckv_p, kpe_p = _od_in_prompt(xp, mp[0], mp[1], w_in_pp, qn, kvn, wuq_p, wuk_p, wuv_p, tq_p, tk_p)
    attn = _flash_prompt(qh, kh, v)
    xp = _proj_res([attn], xp, mp[2], [w_o])

    w_in_ps = jnp.concatenate([w_cq, w_ckv, _pad_cols(w_kpe, LANE)], axis=1).astype(BF16)
    wq_nope = jnp.pad(w_uq[:, :, :C_NOPE], ((0, 0), (0, 0), (0, LANE - C_NOPE))).reshape(Q_LORA, C_HEADS * LANE)
    wq_rope = jnp.pad(w_uq[:, :, C_NOPE:], ((0, 0), (0, 0), (0, LANE - C_ROPE))).reshape(Q_LORA, C_HEADS * LANE)
    wuq_s = jnp.concatenate([wq_nope, wq_rope], axis=1).astype(BF16)
    wukt_s = jnp.pad(jnp.transpose(w_uk, (1, 2, 0)), ((0, 0), (0, LANE - C_NOPE), (0, 0))).astype(BF16)
    pos_s = jnp.tile(past + jnp.arange(ls, dtype=jnp.int32), bs)
    tq_s = _rope_tables(pos_s, 0, 1, MLA_SCALE)
    tk_s = _rope_tables(pos_s, 0, 1, 1.0)
    ql, qp, ckv_s, kpe_s, kpe_pad_s = _od_in_sample(xs, ms[0], ms[1], w_in_ps, qn, kvn, wuq_s, wukt_s, tq_s, tk_s)
    nq = ls * C_HEADS
    o_lat = _paged_attention(page_table, ql.reshape(bs, nq, KV_LORA), qp.reshape(bs, nq, LANE),
                             ckv_s.reshape(bs, ls, KV_LORA), kpe_pad_s.reshape(bs, ls, LANE),
                             cache_ckv[i:i + 1], jnp.swapaxes(cache_kpe[i:i + 1], 2, 3))
    wuv_h = jnp.transpose(w_uv, (1, 0, 2)).reshape(C_HEADS // 2, 2, KV_LORA, C_V)
    zero = jnp.zeros((C_HEADS // 2, KV_LORA, C_V), F32)
    wuv_pairs = jnp.concatenate([jnp.concatenate([wuv_h[:, 0], zero], axis=2),
                                 jnp.concatenate([zero, wuv_h[:, 1]], axis=2)], axis=1).astype(BF16)
    xs = _mla_out_sample(o_lat.reshape(ts, C_HEADS * KV_LORA), xs, ms[2], wuv_pairs, w_o)

    router_pad = _pad_cols(od_router[i].astype(F32), LANE)
    wgu_e = od_w_gu[i].astype(BF16)
    wdn_e = od_w_down[i].astype(BF16)
    fn = final_norm.reshape(1, d).astype(F32)
    h_p, gate_p, idx_p = _route(xp, mp[3], mp[4], router_pad)
    h_s, gate_s, idx_s = _route(xs, ms[3], ms[4], router_pad)
    idx_all = jnp.concatenate([idx_p[:, :2], idx_s[:, :2]], axis=0)
    dest, row_token, tile_expert, n_used = _moe_plan(idx_all, MOE_TM)
    xs_sorted = _moe_gather(row_token, jnp.concatenate([h_p, h_s], axis=0), MOE_TM)
    y_sorted = _moe_grouped(tile_expert, n_used, xs_sorted, wgu_e, wdn_e, MOE_TM)
    y_prompt = _moe_combine(dest[:2 * tp], xp, mp[5], gate_p, fn, y_sorted)
    y_sample = _moe_combine(dest[2 * tp:], xs, ms[5], gate_s, fn, y_sorted)

    return (y_prompt.reshape(1, tp, d), y_sample.reshape(bs, ls, d),
            swa_kv_prompt, swa_kv_sample, conv_prompt, conv_sample, ssm_prompt, ssm_sample,
            ckv_p.reshape(1, 1, tp, KV_LORA), ckv_s.reshape(1, bs, ls, KV_LORA),
            kpe_p.reshape(1, 1, tp, C_ROPE), kpe_s.reshape(1, bs, ls, C_ROPE))
```
